```python
import jax, jax.numpy as jnp
from jax import lax
import numpy as np

D_MODEL = 1024
BATCH = 2
SEQ = 8192
DEPTH = 2

EPS = 1e-6
MLA_HEADS = 8
MLA_Q_LORA = 256
MLA_KV_LORA = 256
MLA_NOPE = 64
MLA_ROPE = 32
MLA_V = 64
MLA_QK = MLA_NOPE + MLA_ROPE
MLA_W = MLA_HEADS * MLA_V
ROPE_THETA = 10000.0
Q_BLOCK = 128
DN_HEADS = 8
DN_DK = 64
DN_DV = 64
DN_KW = DN_HEADS * DN_DK
DN_VW = DN_HEADS * DN_DV
DN_CONV = 5
DN_CHUNK = 64
MEM_TOKENS = 256
MEM_HEADS = 4
MEM_HD = 128
MEM_W = MEM_HEADS * MEM_HD
N_BRANCH = 3
D_FF = -(-8 * D_MODEL // (3 * 256)) * 256
IN_SIZES = (MLA_Q_LORA, MLA_KV_LORA, MLA_ROPE,
            DN_KW, DN_KW, DN_VW, DN_VW,
            2 * DN_HEADS, 2 * DN_HEADS,
            MEM_W,
            N_BRANCH * D_MODEL)
IN_COLS = sum(IN_SIZES)

kernel_name = "hybrid_mla_gdn_mem_encoder"


def split_last(t, sizes):
    out, off = [], 0
    for s in sizes:
        out.append(t[..., off:off + s])
        off += s
    return out


def rmsnorm(x, w):
    xf = x.astype(jnp.float32)
    y = xf * lax.rsqrt(jnp.mean(xf * xf, axis=-1, keepdims=True) + EPS)
    return (y * w.astype(jnp.float32)).astype(x.dtype)


def l2norm(x):
    return x * lax.rsqrt(jnp.sum(x * x, axis=-1, keepdims=True) + EPS)


def rope_tables(positions):
    inv_freq = 1.0 / (ROPE_THETA ** (jnp.arange(0, MLA_ROPE, 2, dtype=jnp.float32) / MLA_ROPE))
    ang = positions.astype(jnp.float32)[..., None] * inv_freq
    return jnp.cos(ang), jnp.sin(ang)


def apply_rope(x, cos, sin):
    xf = x.astype(jnp.float32)
    x1, x2 = xf[..., : MLA_ROPE // 2], xf[..., MLA_ROPE // 2:]
    return jnp.concatenate([x1 * cos - x2 * sin, x1 * sin + x2 * cos], axis=-1).astype(x.dtype)


def mla_branch(c_q, c_kv, k_rope_raw, cos, sin, q_norm, w_uq, kv_norm, w_ukv):
    B, S, _ = c_q.shape
    q = (rmsnorm(c_q, q_norm) @ w_uq).reshape(B, S, MLA_HEADS, MLA_QK)
    q_nope = q[..., :MLA_NOPE]
    q_rope = apply_rope(q[..., MLA_NOPE:], cos[:, :, None, :], sin[:, :, None, :])
    kv = (rmsnorm(c_kv, kv_norm) @ w_ukv).reshape(B, S, MLA_HEADS, MLA_NOPE + MLA_V)
    k_nope, v = kv[..., :MLA_NOPE], kv[..., MLA_NOPE:]
    k_rope = apply_rope(k_rope_raw, cos, sin)
    scale = MLA_QK ** -0.5
    nb = S // Q_BLOCK
    qn_b = q_nope.reshape(B, nb, Q_BLOCK, MLA_HEADS, MLA_NOPE).swapaxes(0, 1)
    qr_b = q_rope.reshape(B, nb, Q_BLOCK, MLA_HEADS, MLA_ROPE).swapaxes(0, 1)

    def block(args):
        qn, qr = args
        s = (jnp.einsum('bqhd,bkhd->bhqk', qn, k_nope)
             + jnp.einsum('bqhr,bkr->bhqk', qr, k_rope))
        p = jax.nn.softmax(s.astype(jnp.float32) * scale, axis=-1).astype(v.dtype)
        return jnp.einsum('bhqk,bkhd->bqhd', p, v)

    o = lax.map(block, (qn_b, qr_b))
    return o.swapaxes(0, 1).reshape(B, S, MLA_W)


def short_conv(x, w):
    C = x.shape[-1]
    y = lax.conv_general_dilated(
        x, w[:, None, :].astype(x.dtype), window_strides=(1,),
        padding=[(DN_CONV // 2, DN_CONV // 2)],
        dimension_numbers=('NWC', 'WIO', 'NWC'), feature_group_count=C)
    return jax.nn.silu(y)


def chunk_gated_delta(q, k, v, g, beta):
    B, S, H, DK = q.shape
    DV = v.shape[-1]
    C = DN_CHUNK
    N = S // C
    to_c = lambda t: jnp.moveaxis(t.reshape((B, N, C, H) + t.shape[3:]), 3, 1)
    q, k, v, g, beta = to_c(q), to_c(k), to_c(v), to_c(g), to_c(beta)
    gc = jnp.cumsum(g, axis=-1)
    kb = k * beta[..., None]
    vb = v * beta[..., None]
    idx = jnp.arange(C)
    incl = idx[:, None] >= idx[None, :]
    strict = idx[:, None] > idx[None, :]
    decay = jnp.exp(jnp.where(incl, gc[..., :, None] - gc[..., None, :], -jnp.inf))
    L = jnp.where(strict, jnp.einsum('bhnid,bhnjd->bhnij', kb, k) * decay, 0.0)
    A = L + jnp.eye(C, dtype=jnp.float32)
    tri = lambda rhs: lax.linalg.triangular_solve(A, rhs, left_side=True, lower=True,
                                                  unit_diagonal=True)
    u = tri(vb)
    w = tri(kb * jnp.exp(gc)[..., None])
    attn = jnp.einsum('bhnid,bhnjd->bhnij', q, k) * decay

    def step(state, inp):
        q_i, k_i, u_i, w_i, gc_i, attn_i = inp
        v_new = u_i - jnp.einsum('bhcd,bhde->bhce', w_i, state)
        o = (jnp.einsum('bhcd,bhde->bhce', q_i * jnp.exp(gc_i)[..., None], state)
             + jnp.einsum('bhij,bhje->bhie', attn_i, v_new))
        g_last = gc_i[..., -1]
        k_dec = k_i * jnp.exp(g_last[..., None] - gc_i)[..., None]
        state = state * jnp.exp(g_last)[..., None, None] + jnp.einsum('bhcd,bhce->bhde', k_dec, v_new)
        return state, o

    xs = tuple(jnp.moveaxis(t, 2, 0) for t in (q, k, u, w, gc, attn))
    s0 = jnp.zeros((B, H, DK, DV), jnp.float32)
    _, o = lax.scan(step, s0, xs)
    return jnp.moveaxis(o, 0, 2).swapaxes(1, 3).reshape(B, S, H, DV)


def deltanet_branch(dq, dk, dv, dz, da, db, conv_w, a_log, dt_bias, out_norm):
    B, S, _ = dq.shape
    qkv = short_conv(jnp.concatenate([dq, dk, dv], axis=-1), conv_w)
    q, k, v = split_last(qkv, (DN_KW, DN_KW, DN_VW))
    q = l2norm(q.reshape(B, S, DN_HEADS, DN_DK).astype(jnp.float32)) * (DN_DK ** -0.5)
    k = l2norm(k.reshape(B, S, DN_HEADS, DN_DK).astype(jnp.float32))
    v = v.reshape(B, S, DN_HEADS, DN_DV).astype(jnp.float32)
    a = da.reshape(B, S, 2, DN_HEADS).astype(jnp.float32)
    b = db.reshape(B, S, 2, DN_HEADS).astype(jnp.float32)
    g = -jnp.exp(a_log.astype(jnp.float32)) * jax.nn.softplus(a + dt_bias.astype(jnp.float32))
    beta = jax.nn.sigmoid(b)
    o_f = chunk_gated_delta(q, k, v, g[:, :, 0], beta[:, :, 0])
    rev = lambda t: jnp.flip(t, axis=1)
    o_b = rev(chunk_gated_delta(rev(q), rev(k), rev(v), rev(g[:, :, 1]), rev(beta[:, :, 1])))
    o = o_f + o_b
    z = dz.reshape(B, S, DN_HEADS, DN_DV).astype(jnp.float32)
    o = (o * lax.rsqrt(jnp.mean(o * o, axis=-1, keepdims=True) + EPS)
         * out_norm.astype(jnp.float32) * jax.nn.silu(z))
    return o.reshape(B, S, DN_VW).astype(dq.dtype)


def mem_branch(mq, mem, mem_norm_w, w_kv):
    B, S, _ = mq.shape
    kv = rmsnorm(mem, mem_norm_w) @ w_kv
    k = kv[..., :MEM_W].reshape(B, -1, MEM_HEADS, MEM_HD)
    v = kv[..., MEM_W:].reshape(B, -1, MEM_HEADS, MEM_HD)
    q = mq.reshape(B, S, MEM_HEADS, MEM_HD)
    s = jnp.einsum('bshd,bmhd->bhsm', q, k).astype(jnp.float32) * (MEM_HD ** -0.5)
    p = jax.nn.softmax(s, axis=-1).astype(v.dtype)
    return jnp.einsum('bhsm,bmhd->bshd', p, v).reshape(B, S, MEM_W)


def setup_inputs(seed: int = 0) -> dict:
    key = jax.random.key(seed)
    ks = iter(jax.random.split(key, 40))
    f32 = jnp.float32
    nrm = lambda shape, fan_in: jax.random.normal(next(ks), shape, f32) * (fan_in ** -0.5)
    gain = lambda shape: 1.0 + 0.02 * jax.random.normal(next(ks), shape, f32)
    L = DEPTH
    x = jax.random.normal(next(ks), (BATCH, SEQ, D_MODEL), f32)
    mem = jax.random.normal(next(ks), (BATCH, MEM_TOKENS, D_MODEL), f32)
    positions = (jnp.arange(SEQ, dtype=jnp.int32)[None, :]
                 + jax.random.randint(next(ks), (BATCH, 1), 0, 4096, dtype=jnp.int32))
    a_log = jnp.log(jax.random.uniform(next(ks), (L, 2, DN_HEADS), f32, 1.0, 16.0))
    dt = jnp.exp(jax.random.uniform(next(ks), (L, 2, DN_HEADS), f32,
                                    float(np.log(1e-3)), float(np.log(1e-1))))
    dt_bias = dt + jnp.log(-jnp.expm1(-dt))
    return {
        "x": x,
        "mem": mem,
        "positions": positions,
        "norm_mix": gain((L, D_MODEL)),
        "w_in": nrm((L, D_MODEL, IN_COLS), D_MODEL),
        "mla_q_norm": gain((L, MLA_Q_LORA)),
        "mla_w_uq": nrm((L, MLA_Q_LORA, MLA_HEADS * MLA_QK), MLA_Q_LORA),
        "mla_kv_norm": gain((L, MLA_KV_LORA)),
        "mla_w_ukv": nrm((L, MLA_KV_LORA, MLA_HEADS * (MLA_NOPE + MLA_V)), MLA_KV_LORA),
        "dn_conv": nrm((L, DN_CONV, 2 * DN_KW + DN_VW), DN_CONV),
        "dn_a_log": a_log,
        "dn_dt_bias": dt_bias,
        "dn_out_norm": gain((L, DN_DV)),
        "mem_norm": gain((L, D_MODEL)),
        "mem_w_kv": nrm((L, D_MODEL, 2 * MEM_W), D_MODEL),
        "w_branch_mla": nrm((L, MLA_W, D_MODEL), MLA_W),
        "w_branch_dn": nrm((L, DN_VW, D_MODEL), DN_VW),
        "w_branch_mem": nrm((L, MEM_W, D_MODEL), MEM_W),
        "w_out": nrm((L, D_MODEL, D_MODEL), D_MODEL),
        "norm_ffn": gain((L, D_MODEL)),
        "ffn_w_gate_up": nrm((L, D_MODEL, 2 * D_FF), D_MODEL),
        "ffn_w_down": nrm((L, D_FF, D_MODEL), D_FF),
        "final_norm": gain((D_MODEL,)),
    }


def reference(x, mem, positions, norm_mix, w_in, mla_q_norm, mla_w_uq, mla_kv_norm, mla_w_ukv,
              dn_conv, dn_a_log, dn_dt_bias, dn_out_norm, mem_norm, mem_w_kv,
              w_branch_mla, w_branch_dn, w_branch_mem, w_out, norm_ffn,
              ffn_w_gate_up, ffn_w_down, final_norm):
    B, S, D = x.shape
    cos, sin = rope_tables(positions)
    for l in range(DEPTH):
        h = rmsnorm(x, norm_mix[l])
        proj = h @ w_in[l]
        (c_q, c_kv, k_rope_raw, dq, dk, dv, dz, da, db, mq,
         gate_logits) = split_last(proj, IN_SIZES)
        o_mla = mla_branch(c_q, c_kv, k_rope_raw, cos, sin,
                           mla_q_norm[l], mla_w_uq[l], mla_kv_norm[l], mla_w_ukv[l])
        o_dn = deltanet_branch(dq, dk, dv, dz, da, db, dn_conv[l], dn_a_log[l],
                               dn_dt_bias[l], dn_out_norm[l])
        o_mem = mem_branch(mq, mem, mem_norm[l], mem_w_kv[l])
        gates = jax.nn.sigmoid(gate_logits.astype(jnp.float32)).astype(x.dtype)
        gates = gates.reshape(B, S, N_BRANCH, D)
        merged = (gates[:, :, 0] * (o_mla @ w_branch_mla[l])
                  + gates[:, :, 1] * (o_dn @ w_branch_dn[l])
                  + gates[:, :, 2] * (o_mem @ w_branch_mem[l]))
        x = x + merged @ w_out[l]
        h = rmsnorm(x, norm_ffn[l])
        gu = h @ ffn_w_gate_up[l]
        x = x + (jax.nn.silu(gu[..., :D_FF]) * gu[..., D_FF:]) @ ffn_w_down[l]
    return rmsnorm(x, final_norm)
```

```python
import functools

import jax
import jax.numpy as jnp
from jax import lax
from jax.experimental import pallas as pl
from jax.experimental.pallas import tpu as pltpu

F32 = jnp.float32
BF16 = jnp.bfloat16

D_MODEL = 1024
EPS = 1e-6
MLA_HEADS = 8
MLA_Q_LORA = 256
MLA_KV_LORA = 256
MLA_NOPE = 64
MLA_ROPE = 32
MLA_V = 64
MLA_QK = MLA_NOPE + MLA_ROPE
ROPE_THETA = 10000.0
DN_HEADS = 8
DN_DK = 64
DN_W = DN_HEADS * DN_DK
DN_CONV = 5
DN_CHUNK = 64
MEM_HEADS = 4
MEM_HD = 128
MEM_W = MEM_HEADS * MEM_HD
D_FF = 2816

COL_GATES = 0
COL_DQKV = 3072
COL_DZ = 4608
COL_MQ = 5120
COL_CQ = 5632
COL_CKV = 5888
COL_SMALL = 6144
COL_SMALL2 = 6272
IN_COLS_P = 6400

HEAD_SLAB = 128
VT_ROWS = 80
LOG2E = 1.4426950408889634
VMEM_LIMIT = 48 * 1024 * 1024


def _cparams(sem):
    return pltpu.CompilerParams(dimension_semantics=sem, vmem_limit_bytes=VMEM_LIMIT)


def _sigmoid(x):
    return 1.0 / (1.0 + jnp.exp(-x))


def _dot(a, b):
    return jnp.dot(a, b, preferred_element_type=F32)


def _dot_nt(a, b):
    return lax.dot_general(a, b, (((1,), (1,)), ((), ())), preferred_element_type=F32)


def _dot_tn(a, b):
    return lax.dot_general(a, b, (((0,), (0,)), ((), ())), preferred_element_type=F32)


def _split3(x):
    x1 = x.astype(BF16)
    r1 = x - x1.astype(F32)
    x2 = r1.astype(BF16)
    x3 = (r1 - x2.astype(F32)).astype(BF16)
    return x1, x2, x3


def _rms(x, w):
    return x * lax.rsqrt(jnp.mean(x * x, axis=-1, keepdims=True) + EPS) * w


def _norm_matmul_kernel(x_ref, g_ref, w_ref, o_ref, h_ref):
    @pl.when(pl.program_id(1) == 0)
    def _():
        h_ref[...] = _rms(x_ref[...], g_ref[...]).astype(BF16)

    o_ref[...] = _dot(h_ref[...], w_ref[...]).astype(o_ref.dtype)


def _norm_matmul(x, g, w, *, tm, tn, out_dtype):
    t, d = x.shape
    n = w.shape[1]
    return pl.pallas_call(
        _norm_matmul_kernel,
        grid=(t // tm, n // tn),
        in_specs=[pl.BlockSpec((tm, d), lambda i, j: (i, 0)),
                  pl.BlockSpec((1, d), lambda i, j: (0, 0)),
                  pl.BlockSpec((d, tn), lambda i, j: (0, j))],
        out_specs=pl.BlockSpec((tm, tn), lambda i, j: (i, j)),
        out_shape=jax.ShapeDtypeStruct((t, n), out_dtype),
        scratch_shapes=[pltpu.VMEM((tm, d), BF16)],
        compiler_params=_cparams(("parallel", "arbitrary")),
        name="norm_matmul",
    )(x, g, w)


def _mla_prep_kernel(cq_ref, ckv_ref, sm_ref, ck_ref, sk_ref, cqt_ref, sqt_ref, qn_ref, kvn_ref,
                     wq1t_ref, wq2t_ref, wk_ref, wvt_ref, qt_out, k_out, vt_out):
    qn = _rms(cq_ref[...], qn_ref[...]).astype(BF16)
    q1t = _dot_nt(wq1t_ref[...], qn)
    q2t = _dot_nt(wq2t_ref[...], qn)
    ct = cqt_ref[0]
    st = sqt_ref[0]
    for h in range(MLA_HEADS):
        hs = slice(h * HEAD_SLAB, (h + 1) * HEAD_SLAB)
        qt_out[0, hs, :] = (q1t[hs] * ct + q2t[hs] * st).astype(BF16)

    kvn = _rms(ckv_ref[...], kvn_ref[...]).astype(BF16)
    kk = _dot(kvn, wk_ref[...])
    sm = sm_ref[...]
    kr = sm[:, :128] * ck_ref[...] + sm[:, 128:] * sk_ref[...]
    for h in range(MLA_HEADS):
        hs = slice(h * HEAD_SLAB, (h + 1) * HEAD_SLAB)
        k_out[:, hs] = (kk[:, hs] + kr).astype(BF16)

    vt = _dot_nt(wvt_ref[...], kvn)
    tm = vt.shape[1]
    ones = jnp.ones((VT_ROWS - MLA_V, tm), BF16)
    for h in range(MLA_HEADS):
        vt_out[0, h * VT_ROWS:h * VT_ROWS + MLA_V, :] = vt[h * MLA_V:(h + 1) * MLA_V].astype(BF16)
        vt_out[0, h * VT_ROWS + MLA_V:(h + 1) * VT_ROWS, :] = ones


def _mla_prep(proj, ck, sk, cqt, sqt, qnorm, kvnorm, wq1t, wq2t, wk, wvt, *, b, s, tm):
    nb = s // tm
    t = b * s
    row = lambda bi, i: bi * nb + i
    const = lambda shape: pl.BlockSpec(shape, lambda bi, i: (0,) * len(shape))
    return pl.pallas_call(
        _mla_prep_kernel,
        grid=(b, nb),
        in_specs=[pl.BlockSpec((tm, 256), lambda bi, i: (row(bi, i), COL_CQ // 256)),
                  pl.BlockSpec((tm, 256), lambda bi, i: (row(bi, i), COL_CKV // 256)),
                  pl.BlockSpec((tm, 256), lambda bi, i: (row(bi, i), COL_SMALL // 256)),
                  pl.BlockSpec((tm, 128), lambda bi, i: (row(bi, i), 0)),
                  pl.BlockSpec((tm, 128), lambda bi, i: (row(bi, i), 0)),
                  pl.BlockSpec((1, 128, tm), lambda bi, i: (bi, 0, i)),
                  pl.BlockSpec((1, 128, tm), lambda bi, i: (bi, 0, i)),
                  const((1, 256)), const((1, 256)),
                  const((1024, 256)), const((1024, 256)), const((256, 1024)), const((512, 256))],
        out_specs=[pl.BlockSpec((1, 1024, tm), lambda bi, i: (bi, 0, i)),
                   pl.BlockSpec((tm, 1024), lambda bi, i: (row(bi, i), 0)),
                   pl.BlockSpec((1, MLA_HEADS * VT_ROWS, tm), lambda bi, i: (bi, 0, i))],
        out_shape=[jax.ShapeDtypeStruct((b, 1024, s), BF16),
                   jax.ShapeDtypeStruct((t, 1024), BF16),
                   jax.ShapeDtypeStruct((b, MLA_HEADS * VT_ROWS, s), BF16)],
        compiler_params=_cparams(("parallel", "parallel")),
        name="mla_prep",
    )(proj, proj, proj, ck, sk, cqt, sqt, qnorm, kvnorm, wq1t, wq2t, wk, wvt)


def _flash_kernel(qt_ref, k_ref, vt_ref, o_ref, m_ref, acc_ref, *, tk, nk):
    m_ref[...] = jnp.full(m_ref.shape, -jnp.inf, F32)
    acc_ref[...] = jnp.zeros(acc_ref.shape, F32)

    def body(j, carry):
        r0 = pl.multiple_of(j * tk, tk)
        kc = k_ref[pl.ds(r0, tk), :]
        vc = vt_ref[0, :, pl.ds(r0, tk)]
        for h in range(2):
            qt = qt_ref[0, h * HEAD_SLAB:(h + 1) * HEAD_SLAB, :]
            st = _dot(kc[:, h * HEAD_SLAB:(h + 1) * HEAD_SLAB], qt)
            m_old = m_ref[h]
            m_new = jnp.maximum(m_old, jnp.max(st, axis=0, keepdims=True))
            alpha = jnp.exp2(m_old - m_new)
            pt = jnp.exp2(st - m_new).astype(BF16)
            pv = _dot(vc[h * VT_ROWS:(h + 1) * VT_ROWS, :], pt)
            acc_ref[h] = alpha * acc_ref[h] + pv
            m_ref[h] = m_new
        return carry

    lax.fori_loop(0, nk, body, 0)
    outs = []
    for h in range(2):
        a = acc_ref[h]
        outs.append(a[0:MLA_V] / a[MLA_V:MLA_V + 1])
    ot = jnp.concatenate(outs, axis=0)
    o_ref[...] = ot.T.astype(o_ref.dtype)


def _flash(qt, k, vt, *, b, s, tq, tk):
    nq = s // tq
    nk = s // tk
    hp = MLA_HEADS // 2
    return pl.pallas_call(
        functools.partial(_flash_kernel, tk=tk, nk=nk),
        grid=(b, hp, nq),
        in_specs=[pl.BlockSpec((1, 2 * HEAD_SLAB, tq), lambda bi, p, qi: (bi, p, qi)),
                  pl.BlockSpec((s, 2 * HEAD_SLAB), lambda bi, p, qi: (bi, p)),
                  pl.BlockSpec((1, 2 * VT_ROWS, s), lambda bi, p, qi: (bi, p, 0))],
        out_specs=pl.BlockSpec((tq, 2 * MLA_V), lambda bi, p, qi: (bi * nq + qi, p)),
        out_shape=jax.ShapeDtypeStruct((b * s, MLA_HEADS * MLA_V), BF16),
        scratch_shapes=[pltpu.VMEM((2, 1, tq), F32), pltpu.VMEM((2, VT_ROWS, tq), F32)],
        compiler_params=_cparams(("parallel", "parallel", "arbitrary")),
        name="mla_flash",
    )(qt, k, vt)


def _group_sum(z, gmat):
    z1 = z.astype(BF16)
    z2 = (z - z1.astype(F32)).astype(BF16)
    return _dot(z1, gmat) + _dot(z2, gmat)


def _dn_prep_kernel(x_ref, xp_ref, xn_ref, sm_ref, cw_ref, par_ref, gm_ref,
                    q_out, k_out, v_out, gb_out, pad_ref, *, nb, tm):
    i = pl.program_id(0)
    first = (i % nb) == 0
    last = (i % nb) == nb - 1
    pad_ref[0:8, :] = jnp.where(first, 0.0, xp_ref[...])
    pad_ref[8:8 + tm, :] = x_ref[...]
    pad_ref[8 + tm:16 + tm, :] = jnp.where(last, 0.0, xn_ref[...])
    cw = cw_ref[...]
    y = cw[0:1] * pad_ref[pl.ds(6, tm), :]
    for j in range(1, DN_CONV):
        y = y + cw[j:j + 1] * pad_ref[pl.ds(6 + j, tm), :]
    y = y * _sigmoid(y)
    gm = gm_ref[...]
    q = y[:, 0:DN_W]
    k = y[:, DN_W:2 * DN_W]
    q_out[...] = q * lax.rsqrt(_group_sum(q * q, gm) + EPS) * (DN_DK ** -0.5)
    k_out[...] = k * lax.rsqrt(_group_sum(k * k, gm) + EPS)
    v_out[...] = y[:, 2 * DN_W:]

    sm = sm_ref[...]
    par = par_ref[...]
    z = sm + par[1:2]
    softplus = jnp.maximum(z, 0.0) + jnp.log1p(jnp.exp(-jnp.abs(z)))
    g = -jnp.exp(par[0:1]) * softplus
    lane = lax.broadcasted_iota(jnp.int32, sm.shape, 1)
    gb_out[...] = jnp.where(lane < 16, g, _sigmoid(sm))


def _dn_prep(proj, conv_w, par, gmat, *, b, s, tm):
    t = b * s
    nb = s // tm
    w3 = 3 * DN_W
    cb = COL_DQKV // w3
    hb = tm // 8
    last_hb = t // 8 - 1
    return pl.pallas_call(
        functools.partial(_dn_prep_kernel, nb=nb, tm=tm),
        grid=(t // tm,),
        in_specs=[pl.BlockSpec((tm, w3), lambda i: (i, cb)),
                  pl.BlockSpec((8, w3), lambda i: (jnp.maximum(i * hb - 1, 0), cb)),
                  pl.BlockSpec((8, w3), lambda i: (jnp.minimum((i + 1) * hb, last_hb), cb)),
                  pl.BlockSpec((tm, 128), lambda i: (i, COL_SMALL // 128)),
                  pl.BlockSpec((8, w3), lambda i: (0, 0)),
                  pl.BlockSpec((8, 128), lambda i: (0, 0)),
                  pl.BlockSpec((DN_W, DN_W), lambda i: (0, 0))],
        out_specs=[pl.BlockSpec((tm, DN_W), lambda i: (i, 0)),
                   pl.BlockSpec((tm, DN_W), lambda i: (i, 0)),
                   pl.BlockSpec((tm, DN_W), lambda i: (i, 0)),
                   pl.BlockSpec((tm, 128), lambda i: (i, 0))],
        out_shape=[jax.ShapeDtypeStruct((t, DN_W), F32)] * 3 + [jax.ShapeDtypeStruct((t, 128), F32)],
        scratch_shapes=[pltpu.VMEM((tm + 16, w3), F32)],
        compiler_params=_cparams(("parallel",)),
        name="dn_prep",
    )(proj, proj, proj, proj, conv_w, par, gmat)


GRP = 256
NGRP = DN_W // GRP


def _bd_rows(x_bf, bmask):
    return jnp.where(bmask, jnp.concatenate([x_bf] * (GRP // DN_CHUNK), axis=0), jnp.zeros((), BF16))


def _dn_chunk_kernel(qf_ref, kf_ref, vf_ref, gf_ref, qb_ref, kb_ref, vb_ref, gb_ref,
                     of_ref, ob_ref, s_ref):
    c = DN_CHUNK

    @pl.when(pl.program_id(1) == 0)
    def _():
        s_ref[...] = jnp.zeros(s_ref.shape, F32)

    ri = lax.broadcasted_iota(jnp.int32, (c, GRP), 0)
    cj = lax.broadcasted_iota(jnp.int32, (c, GRP), 1) & (c - 1)
    br = lax.broadcasted_iota(jnp.int32, (GRP, GRP), 0) >> 6
    bc = lax.broadcasted_iota(jnp.int32, (GRP, GRP), 1) >> 6
    bmask = br == bc
    ident = (ri == cj).astype(F32)

    def pmm(a, bmat):
        return _dot(a.astype(BF16), _bd_rows(bmat.astype(BF16), bmask))

    def pmm_nt(a, bmat):
        return _dot_nt(a.astype(BF16), _bd_rows(bmat.astype(BF16), bmask))

    def inverse(lmat):
        d = jnp.where((ri >> 3) == (cj >> 3), lmat, 0.0)
        d2 = pmm(d, d)
        d4 = pmm(d2, d2)
        x = pmm(ident - d, ident + d2)
        x = pmm(x, ident + d4)
        for sh in (3, 4, 5):
            off = ((ri >> (sh + 1)) == (cj >> (sh + 1))) & ((ri >> sh) != (cj >> sh))
            x = x - pmm(pmm(x, jnp.where(off, lmat, 0.0)), x)
        return x

    er = lax.broadcasted_iota(jnp.int32, (128, DN_W), 0)
    ec = lax.broadcasted_iota(jnp.int32, (128, DN_W), 1) >> 6
    tr = lax.broadcasted_iota(jnp.int32, (c, c), 0)
    tc = lax.broadcasted_iota(jnp.int32, (c, c), 1)
    mr = lax.broadcasted_iota(jnp.int32, (c, DN_W), 0)
    mj = lax.broadcasted_iota(jnp.int32, (c, DN_W), 1) & (c - 1)
    ones_cc = jnp.ones((c, c), BF16)

    for d, (q_ref, k_ref, v_ref, g_ref, o_ref) in enumerate(
            ((qf_ref, kf_ref, vf_ref, gf_ref, of_ref), (qb_ref, kb_ref, vb_ref, gb_ref, ob_ref))):
        q = q_ref[...]
        k = k_ref[...]
        v = v_ref[...]
        g3 = _split3(g_ref[...])
        e_g = (er == d * DN_HEADS + ec).astype(BF16)
        e_b = (er == 16 + d * DN_HEADS + ec).astype(BF16)
        gexp = _dot(g3[0], e_g) + _dot(g3[1], e_g) + _dot(g3[2], e_g)
        bexp = _dot(g3[0], e_b) + _dot(g3[1], e_b) + _dot(g3[2], e_b)
        if d == 0:
            tri = (tc <= tr).astype(BF16)
            mask_t = mr <= mj
            incl = ri >= cj
            strict = ri > cj
        else:
            tri = (tc >= tr).astype(BF16)
            mask_t = mr >= mj
            incl = ri <= cj
            strict = ri < cj
        ge3 = _split3(gexp)
        gcrow = _dot(tri, ge3[0]) + _dot(tri, ge3[1]) + _dot(tri, ge3[2])
        gm3 = _split3(jnp.where(mask_t, gexp, 0.0))
        gccol = _dot(ones_cc, gm3[0]) + _dot(ones_cc, gm3[1]) + _dot(ones_cc, gm3[2])
        for g in range(NGRP):
            sl = slice(g * GRP, (g + 1) * GRP)
            kg = k[:, sl]
            qg = q[:, sl]
            bg = bexp[:, sl]
            gc = gcrow[:, sl]
            dec = jnp.exp(jnp.where(incl, gc - gccol[:, sl], -jnp.inf))
            egc = jnp.exp(gc)
            kk = pmm_nt(kg, kg)
            qk = pmm_nt(qg, kg)
            x = inverse(jnp.where(strict, kk * bg * dec, 0.0))
            u = pmm(x, v[:, sl] * bg)
            w = pmm(x, kg * bg * egc)
            st = s_ref[d * NGRP + g]
            sb = st.astype(BF16)
            vnew = u - _dot(w.astype(BF16), sb)
            att = jnp.where(incl, qk * dec, 0.0)
            o_ref[:, sl] = _dot((qg * egc).astype(BF16), sb) + pmm(att, vnew)
            gl = gc[c - 1:c] if d == 0 else gc[0:1]
            kdec = kg * jnp.exp(gl - gc)
            upd = _dot_tn(kdec.astype(BF16), vnew.astype(BF16))
            s_ref[d * NGRP + g] = st * jnp.exp(gl) + jnp.where(bmask, upd, 0.0)


def _dn_chunk(qn, kn, vv, gb, *, b, s):
    n = s // DN_CHUNK
    t = b * s
    fwd = lambda bi, ci: (bi * n + ci, 0)
    bwd = lambda bi, ci: (bi * n + n - 1 - ci, 0)
    wide = lambda im: pl.BlockSpec((DN_CHUNK, DN_W), im)
    small = lambda im: pl.BlockSpec((DN_CHUNK, 128), im)
    return pl.pallas_call(
        _dn_chunk_kernel,
        grid=(b, n),
        in_specs=[wide(fwd), wide(fwd), wide(fwd), small(fwd), wide(bwd), wide(bwd), wide(bwd), small(bwd)],
        out_specs=[wide(fwd), wide(bwd)],
        out_shape=[jax.ShapeDtypeStruct((t, DN_W), F32)] * 2,
        scratch_shapes=[pltpu.VMEM((2 * NGRP, GRP, GRP), F32)],
        compiler_params=_cparams(("parallel", "arbitrary")),
        name="dn_chunk",
    )(qn, kn, vv, gb, qn, kn, vv, gb)


def _merge_kernel(x_ref, gates_ref, dz_ref, mq_ref, omla_ref, of_ref, ob_ref, mk_ref, mv_ref,
                  onorm_ref, gm_ref, wm_ref, wd_ref, wmem_ref, wout_ref, o_ref):
    o = of_ref[...] + ob_ref[...]
    ms = _group_sum(o * o, gm_ref[...]) * (1.0 / DN_DK)
    dz = dz_ref[...]
    o_dn = o * lax.rsqrt(ms + EPS) * onorm_ref[...] * (dz * _sigmoid(dz))

    mq = mq_ref[...]
    mk = mk_ref[0]
    mv = mv_ref[0]
    outs = []
    for h in range(MEM_HEADS):
        hs = slice(h * MEM_HD, (h + 1) * MEM_HD)
        sc = _dot_nt(mq[:, hs].astype(BF16), mk[:, hs]) * (MEM_HD ** -0.5)
        p = jnp.exp(sc - jnp.max(sc, axis=-1, keepdims=True))
        p = p / jnp.sum(p, axis=-1, keepdims=True)
        outs.append(_dot(p.astype(BF16), mv[:, hs]))
    o_mem = jnp.concatenate(outs, axis=-1)

    gates = gates_ref[...]
    merged = (_sigmoid(gates[:, 0:D_MODEL]) * _dot(omla_ref[...], wm_ref[...])
              + _sigmoid(gates[:, D_MODEL:2 * D_MODEL]) * _dot(o_dn.astype(BF16), wd_ref[...])
              + _sigmoid(gates[:, 2 * D_MODEL:]) * _dot(o_mem.astype(BF16), wmem_ref[...]))
    o_ref[...] = x_ref[...] + _dot(merged.astype(BF16), wout_ref[...])


def _merge(x, proj, omla, o_f, o_b, memkv, onorm, gmat, wm, wd, wmem, wout, *, b, s, tm):
    t = b * s
    nb = s // tm
    mt = memkv.shape[1]
    const = lambda shape: pl.BlockSpec(shape, lambda i: (0,) * len(shape))
    return pl.pallas_call(
        _merge_kernel,
        grid=(t // tm,),
        in_specs=[pl.BlockSpec((tm, D_MODEL), lambda i: (i, 0)),
                  pl.BlockSpec((tm, 3 * D_MODEL), lambda i: (i, COL_GATES // (3 * D_MODEL))),
                  pl.BlockSpec((tm, DN_W), lambda i: (i, COL_DZ // DN_W)),
                  pl.BlockSpec((tm, MEM_W), lambda i: (i, COL_MQ // MEM_W)),
                  pl.BlockSpec((tm, DN_W), lambda i: (i, 0)),
                  pl.BlockSpec((tm, DN_W), lambda i: (i, 0)),
                  pl.BlockSpec((tm, DN_W), lambda i: (i, 0)),
                  pl.BlockSpec((1, mt, MEM_W), lambda i: (i // nb, 0, 0)),
                  pl.BlockSpec((1, mt, MEM_W), lambda i: (i // nb, 0, 1)),
                  const((1, DN_W)), const((DN_W, DN_W)),
                  const((DN_W, D_MODEL)), const((DN_W, D_MODEL)), const((MEM_W, D_MODEL)),
                  const((D_MODEL, D_MODEL))],
        out_specs=pl.BlockSpec((tm, D_MODEL), lambda i: (i, 0)),
        out_shape=jax.ShapeDtypeStruct((t, D_MODEL), F32),
        compiler_params=_cparams(("parallel",)),
        name="merge",
    )(x, proj, proj, proj, omla, o_f, o_b, memkv, memkv, onorm, gmat, wm, wd, wmem, wout)


def _ffn_kernel(x_ref, nw_ref, wg_ref, wu_ref, wd_ref, fw_ref, o_ref, h_ref, acc_ref, *, nf, final_norm):
    j = pl.program_id(1)

    @pl.when(j == 0)
    def _():
        h_ref[...] = _rms(x_ref[...], nw_ref[...]).astype(BF16)
        acc_ref[...] = jnp.zeros(acc_ref.shape, F32)

    h = h_ref[...]
    gate = _dot(h, wg_ref[...])
    up = _dot(h, wu_ref[...])
    act = (gate * _sigmoid(gate) * up).astype(BF16)
    acc_ref[...] += _dot(act, wd_ref[...])

    @pl.when(j == nf - 1)
    def _():
        y = x_ref[...] + acc_ref[...]
        if final_norm:
            y = _rms(y, fw_ref[...])
        o_ref[...] = y


def _ffn(x, nw, wgu, wd, fw, *, tm, tf, final_norm):
    t = x.shape[0]
    nf = D_FF // tf
    return pl.pallas_call(
        functools.partial(_ffn_kernel, nf=nf, final_norm=final_norm),
        grid=(t // tm, nf),
        in_specs=[pl.BlockSpec((tm, D_MODEL), lambda i, j: (i, 0)),
                  pl.BlockSpec((1, D_MODEL), lambda i, j: (0, 0)),
                  pl.BlockSpec((D_MODEL, tf), lambda i, j: (0, j)),
                  pl.BlockSpec((D_MODEL, tf), lambda i, j: (0, j + nf)),
                  pl.BlockSpec((tf, D_MODEL), lambda i, j: (j, 0)),
                  pl.BlockSpec((1, D_MODEL), lambda i, j: (0, 0))],
        out_specs=pl.BlockSpec((tm, D_MODEL), lambda i, j: (i, 0)),
        out_shape=jax.ShapeDtypeStruct((t, D_MODEL), F32),
        scratch_shapes=[pltpu.VMEM((tm, D_MODEL), BF16), pltpu.VMEM((tm, D_MODEL), F32)],
        compiler_params=_cparams(("parallel", "arbitrary")),
        name="ffn",
    )(x, nw, wgu, wgu, wd, fw)


def _pick(n, pref):
    return pref if n % pref == 0 else n


def _rope_tables(positions):
    b, s = positions.shape
    inv_freq = 1.0 / (ROPE_THETA ** (jnp.arange(0, MLA_ROPE, 2, dtype=F32) / MLA_ROPE))
    ang = positions.astype(F32)[..., None] * inv_freq
    cos = jnp.cos(ang).reshape(b * s, MLA_ROPE // 2)
    sin = jnp.sin(ang).reshape(b * s, MLA_ROPE // 2)
    z64 = jnp.zeros((b * s, MLA_NOPE), F32)
    z32 = jnp.zeros((b * s, 32), F32)
    ck = jnp.concatenate([z64, cos, cos, z32], axis=1)
    sk = jnp.concatenate([z64, sin, sin, z32], axis=1)
    qscale = (MLA_QK ** -0.5) * LOG2E
    cq = jnp.concatenate([jnp.ones_like(z64), cos, cos, z32], axis=1) * qscale
    sq = sk * qscale
    to_t = lambda a: a.reshape(b, s, 128).transpose(0, 2, 1)
    return ck, sk, to_t(cq), to_t(sq)


def _layer_weights(w_in, w_uq, w_ukv):
    d = w_in.shape[0]
    z = lambda n: jnp.zeros((d, n), w_in.dtype)
    kr = w_in[:, 512:544]
    w_re = jnp.concatenate([
        w_in[:, 3136:6208], w_in[:, 544:2080], w_in[:, 2080:2592], w_in[:, 2624:3136],
        w_in[:, 0:256], w_in[:, 256:512],
        w_in[:, 2592:2624], z(32), kr, z(32),
        z(64), -kr[:, 16:32], kr[:, 0:16], z(32)], axis=1).astype(BF16)

    uq = w_uq.reshape(MLA_Q_LORA, MLA_HEADS, MLA_QK)
    zq = lambda n: jnp.zeros((MLA_Q_LORA, MLA_HEADS, n), w_uq.dtype)
    x1, x2 = uq[..., 64:80], uq[..., 80:96]
    wq1 = jnp.concatenate([uq[..., :64], x1, x2, zq(32)], axis=-1).reshape(MLA_Q_LORA, 1024)
    wq2 = jnp.concatenate([zq(64), -x2, x1, zq(32)], axis=-1).reshape(MLA_Q_LORA, 1024)
    ukv = w_ukv.reshape(MLA_KV_LORA, MLA_HEADS, MLA_NOPE + MLA_V)
    wk = jnp.concatenate([ukv[..., :64], jnp.zeros((MLA_KV_LORA, MLA_HEADS, 64), w_ukv.dtype)],
                         axis=-1).reshape(MLA_KV_LORA, 1024)
    wv = ukv[..., 64:].reshape(MLA_KV_LORA, MLA_HEADS * MLA_V)
    return w_re, wq1.T.astype(BF16), wq2.T.astype(BF16), wk.astype(BF16), wv.T.astype(BF16)


def kernel(x, mem, positions, norm_mix, w_in, mla_q_norm, mla_w_uq, mla_kv_norm, mla_w_ukv, dn_conv, dn_a_log, dn_dt_bias, dn_out_norm, mem_norm, mem_w_kv, w_branch_mla, w_branch_dn, w_branch_mem, w_out, norm_ffn, ffn_w_gate_up, ffn_w_down, final_norm):
    b, s, d = x.shape
    t = b * s
    depth = w_in.shape[0]
    mt = mem.shape[1]
    ck, sk, cqt, sqt = _rope_tables(positions)
    gi = jnp.arange(DN_W) // DN_DK
    gmat = (gi[:, None] == gi[None, :]).astype(BF16)
    xf = x.reshape(t, d)
    memf = mem.reshape(b * mt, d)
    tm = _pick(s, 512)
    for l in range(depth):
        w_re, wq1t, wq2t, wk, wvt = _layer_weights(w_in[l], mla_w_uq[l], mla_w_ukv[l])
        proj = _norm_matmul(xf, norm_mix[l][None], w_re, tm=_pick(t, 1024), tn=640, out_dtype=F32)

        qt, kk, vt = _mla_prep(proj, ck, sk, cqt, sqt, mla_q_norm[l][None], mla_kv_norm[l][None],
                               wq1t, wq2t, wk, wvt, b=b, s=s, tm=tm)
        o_mla = _flash(qt, kk, vt, b=b, s=s, tq=_pick(s, 512), tk=_pick(s, 256))

        conv_w = jnp.concatenate([dn_conv[l], jnp.zeros((8 - DN_CONV, 3 * DN_W), F32)], axis=0)
        pad16 = lambda v: jnp.concatenate([v.reshape(-1), jnp.zeros((128 - 2 * DN_HEADS,), F32)])
        par = jnp.concatenate([pad16(dn_a_log[l])[None], pad16(dn_dt_bias[l])[None],
                               jnp.zeros((6, 128), F32)], axis=0)
        qn, kn, vv, gb = _dn_prep(proj, conv_w, par, gmat, b=b, s=s, tm=tm)
        o_f, o_b = _dn_chunk(qn, kn, vv, gb, b=b, s=s)
        n_chunks = s // DN_CHUNK
        perm = lambda a: a.reshape(b, n_chunks, DN_CHUNK, DN_W).swapaxes(1, 2).reshape(t, DN_W)
        o_f, o_b = perm(o_f), perm(o_b)

        memkv = _norm_matmul(memf, mem_norm[l][None], mem_w_kv[l].astype(BF16),
                             tm=b * mt, tn=512, out_dtype=BF16).reshape(b, mt, 2 * MEM_W)
        onorm = jnp.tile(dn_out_norm[l], DN_HEADS)[None]
        xf = _merge(xf, proj, o_mla, o_f, o_b, memkv, onorm, gmat,
                    w_branch_mla[l].astype(BF16), w_branch_dn[l].astype(BF16),
                    w_branch_mem[l].astype(BF16), w_out[l].astype(BF16), b=b, s=s, tm=tm)
        xf = _ffn(xf, norm_ffn[l][None], ffn_w_gate_up[l].astype(BF16), ffn_w_down[l].astype(BF16),
                  final_norm[None], tm=tm, tf=1408, final_norm=(l == depth - 1))
    return xf.reshape(b, s, d)
```

```python
import functools

import jax
import jax.numpy as jnp
from jax import lax
from jax.experimental import pallas as pl
from jax.experimental.pallas import tpu as pltpu

F32 = jnp.float32
BF16 = jnp.bfloat16

D_MODEL = 1024
EPS = 1e-6
MLA_HEADS = 8
MLA_Q_LORA = 256
MLA_KV_LORA = 256
MLA_NOPE = 64
MLA_ROPE = 32
MLA_V = 64
MLA_QK = MLA_NOPE + MLA_ROPE
ROPE_THETA = 10000.0
DN_HEADS = 8
DN_DK = 64
DN_W = DN_HEADS * DN_DK
DN_CONV = 5
DN_CHUNK = 64
MEM_HEADS = 4
MEM_HD = 128
MEM_W = MEM_HEADS * MEM_HD
D_FF = 2816

COL_GATES = 0
COL_DQKV = 3072
COL_DZ = 4608
COL_MQ = 5120
COL_CQ = 5632
COL_CKV = 5888
COL_SMALL = 6144
COL_SMALL2 = 6272
IN_COLS_P = 6400

HEAD_SLAB = 128
VT_ROWS = 80
LOG2E = 1.4426950408889634
VMEM_LIMIT = 48 * 1024 * 1024


def _cparams(sem):
    return pltpu.CompilerParams(dimension_semantics=sem, vmem_limit_bytes=VMEM_LIMIT)


def _sigmoid(x):
    return 1.0 / (1.0 + jnp.exp(-x))


def _dot(a, b):
    return jnp.dot(a, b, preferred_element_type=F32)


def _dot_nt(a, b):
    return lax.dot_general(a, b, (((1,), (1,)), ((), ())), preferred_element_type=F32)


def _dot_tn(a, b):
    return lax.dot_general(a, b, (((0,), (0,)), ((), ())), preferred_element_type=F32)


def _split3(x):
    x1 = x.astype(BF16)
    r1 = x - x1.astype(F32)
    x2 = r1.astype(BF16)
    x3 = (r1 - x2.astype(F32)).astype(BF16)
    return x1, x2, x3


def _rms(x, w):
    return x * lax.rsqrt(jnp.mean(x * x, axis=-1, keepdims=True) + EPS) * w


def _norm_matmul_kernel(x_ref, g_ref, w_ref, o_ref, h_ref):
    @pl.when(pl.program_id(1) == 0)
    def _():
        h_ref[...] = _rms(x_ref[...], g_ref[...]).astype(BF16)

    o_ref[...] = _dot(h_ref[...], w_ref[...]).astype(o_ref.dtype)


def _norm_matmul(x, g, w, *, tm, tn, out_dtype):
    t, d = x.shape
    n = w.shape[1]
    return pl.pallas_call(
        _norm_matmul_kernel,
        grid=(t // tm, n // tn),
        in_specs=[pl.BlockSpec((tm, d), lambda i, j: (i, 0)),
                  pl.BlockSpec((1, d), lambda i, j: (0, 0)),
                  pl.BlockSpec((d, tn), lambda i, j: (0, j))],
        out_specs=pl.BlockSpec((tm, tn), lambda i, j: (i, j)),
        out_shape=jax.ShapeDtypeStruct((t, n), out_dtype),
        scratch_shapes=[pltpu.VMEM((tm, d), BF16)],
        compiler_params=_cparams(("parallel", "arbitrary")),
        name="norm_matmul",
    )(x, g, w)


def _mla_prep_kernel(cq_ref, ckv_ref, sm_ref, ck_ref, sk_ref, cqt_ref, sqt_ref, qn_ref, kvn_ref,
                     wq1t_ref, wq2t_ref, wk_ref, wvt_ref, qt_out, k_out, vt_out):
    qn = _rms(cq_ref[...], qn_ref[...]).astype(BF16)
    q1t = _dot_nt(wq1t_ref[...], qn)
    q2t = _dot_nt(wq2t_ref[...], qn)
    ct = cqt_ref[0]
    st = sqt_ref[0]
    for h in range(MLA_HEADS):
        hs = slice(h * HEAD_SLAB, (h + 1) * HEAD_SLAB)
        qt_out[0, hs, :] = (q1t[hs] * ct + q2t[hs] * st).astype(BF16)

    kvn = _rms(ckv_ref[...], kvn_ref[...]).astype(BF16)
    kk = _dot(kvn, wk_ref[...])
    sm = sm_ref[...]
    kr = sm[:, :128] * ck_ref[...] + sm[:, 128:] * sk_ref[...]
    for h in range(MLA_HEADS):
        hs = slice(h * HEAD_SLAB, (h + 1) * HEAD_SLAB)
        k_out[:, hs] = (kk[:, hs] + kr).astype(BF16)

    vt = _dot_nt(wvt_ref[...], kvn)
    tm = vt.shape[1]
    ones = jnp.ones((VT_ROWS - MLA_V, tm), BF16)
    for h in range(MLA_HEADS):
        vt_out[0, h * VT_ROWS:h * VT_ROWS + MLA_V, :] = vt[h * MLA_V:(h + 1) * MLA_V].astype(BF16)
        vt_out[0, h * VT_ROWS + MLA_V:(h + 1) * VT_ROWS, :] = ones


def _mla_prep(proj, ck, sk, cqt, sqt, qnorm, kvnorm, wq1t, wq2t, wk, wvt, *, b, s, tm):
    nb = s // tm
    t = b * s
    row = lambda bi, i: bi * nb + i
    const = lambda shape: pl.BlockSpec(shape, lambda bi, i: (0,) * len(shape))
    return pl.pallas_call(
        _mla_prep_kernel,
        grid=(b, nb),
        in_specs=[pl.BlockSpec((tm, 256), lambda bi, i: (row(bi, i), COL_CQ // 256)),
                  pl.BlockSpec((tm, 256), lambda bi, i: (row(bi, i), COL_CKV // 256)),
                  pl.BlockSpec((tm, 256), lambda bi, i: (row(bi, i), COL_SMALL // 256)),
                  pl.BlockSpec((tm, 128), lambda bi, i: (row(bi, i), 0)),
                  pl.BlockSpec((tm, 128), lambda bi, i: (row(bi, i), 0)),
                  pl.BlockSpec((1, 128, tm), lambda bi, i: (bi, 0, i)),
                  pl.BlockSpec((1, 128, tm), lambda bi, i: (bi, 0, i)),
                  const((1, 256)), const((1, 256)),
                  const((1024, 256)), const((1024, 256)), const((256, 1024)), const((512, 256))],
        out_specs=[pl.BlockSpec((1, 1024, tm), lambda bi, i: (bi, 0, i)),
                   pl.BlockSpec((tm, 1024), lambda bi, i: (row(bi, i), 0)),
                   pl.BlockSpec((1, MLA_HEADS * VT_ROWS, tm), lambda bi, i: (bi, 0, i))],
        out_shape=[jax.ShapeDtypeStruct((b, 1024, s), BF16),
                   jax.ShapeDtypeStruct((t, 1024), BF16),
                   jax.ShapeDtypeStruct((b, MLA_HEADS * VT_ROWS, s), BF16)],
        compiler_params=_cparams(("parallel", "parallel")),
        name="mla_prep",
    )(proj, proj, proj, ck, sk, cqt, sqt, qnorm, kvnorm, wq1t, wq2t, wk, wvt)


def _flash_kernel(qt_ref, k_ref, vt_ref, o_ref, m_ref, acc_ref, s_buf, p_buf, a_buf, *, tk, nk):
    m_ref[...] = jnp.full(m_ref.shape, -jnp.inf, F32)
    acc_ref[...] = jnp.zeros(acc_ref.shape, F32)

    def scores(j, par):
        r0 = pl.multiple_of(j * tk, tk)
        kc = k_ref[pl.ds(r0, tk), :]
        for h in range(2):
            qt = qt_ref[0, h * HEAD_SLAB:(h + 1) * HEAD_SLAB, :]
            s_buf[par, h] = _dot(kc[:, h * HEAD_SLAB:(h + 1) * HEAD_SLAB], qt)

    def softmax(par):
        for h in range(2):
            st = s_buf[par, h]
            m_old = m_ref[h]
            m_new = jnp.maximum(m_old, jnp.max(st, axis=0, keepdims=True))
            a_buf[par, h] = jnp.exp2(m_old - m_new)
            p_buf[par, h] = jnp.exp2(st - m_new).astype(BF16)
            m_ref[h] = m_new

    def pv(j, par):
        r0 = pl.multiple_of(j * tk, tk)
        vc = vt_ref[0, :, pl.ds(r0, tk)]
        for h in range(2):
            acc_ref[h] = a_buf[par, h] * acc_ref[h] + _dot(vc[h * VT_ROWS:(h + 1) * VT_ROWS, :], p_buf[par, h])

    scores(0, 0)
    scores(1, 1)
    softmax(0)

    def body(i, carry):
        j = 2 * i
        pv(j, 0)
        softmax(1)
        scores(j + 2, 0)
        pv(j + 1, 1)
        softmax(0)
        scores(j + 3, 1)
        return carry

    lax.fori_loop(0, nk // 2 - 1, body, 0)
    pv(nk - 2, 0)
    softmax(1)
    pv(nk - 1, 1)

    outs = []
    for h in range(2):
        a = acc_ref[h]
        outs.append(a[0:MLA_V] / a[MLA_V:MLA_V + 1])
    ot = jnp.concatenate(outs, axis=0)
    o_ref[...] = ot.T.astype(o_ref.dtype)


def _flash(qt, k, vt, *, b, s, tq, tk):
    nq = s // tq
    nk = s // tk
    hp = MLA_HEADS // 2
    return pl.pallas_call(
        functools.partial(_flash_kernel, tk=tk, nk=nk),
        grid=(b, hp, nq),
        in_specs=[pl.BlockSpec((1, 2 * HEAD_SLAB, tq), lambda bi, p, qi: (bi, p, qi)),
                  pl.BlockSpec((s, 2 * HEAD_SLAB), lambda bi, p, qi: (bi, p)),
                  pl.BlockSpec((1, 2 * VT_ROWS, s), lambda bi, p, qi: (bi, p, 0))],
        out_specs=pl.BlockSpec((tq, 2 * MLA_V), lambda bi, p, qi: (bi * nq + qi, p)),
        out_shape=jax.ShapeDtypeStruct((b * s, MLA_HEADS * MLA_V), BF16),
        scratch_shapes=[pltpu.VMEM((2, 1, tq), F32), pltpu.VMEM((2, VT_ROWS, tq), F32),
                        pltpu.VMEM((2, 2, tk, tq), F32), pltpu.VMEM((2, 2, tk, tq), BF16),
                        pltpu.VMEM((2, 2, 1, tq), F32)],
        compiler_params=_cparams(("parallel", "parallel", "arbitrary")),
        name="mla_flash",
    )(qt, k, vt)


def _group_sum(z, gmat):
    z1 = z.astype(BF16)
    z2 = (z - z1.astype(F32)).astype(BF16)
    return _dot(z1, gmat) + _dot(z2, gmat)


def _dn_prep_kernel(x_ref, xp_ref, xn_ref, sm_ref, cw_ref, par_ref, gm_ref,
                    q_out, k_out, v_out, gb_out, pad_ref, *, nb, tm):
    i = pl.program_id(0)
    first = (i % nb) == 0
    last = (i % nb) == nb - 1
    pad_ref[0:8, :] = jnp.where(first, 0.0, xp_ref[...])
    pad_ref[8:8 + tm, :] = x_ref[...]
    pad_ref[8 + tm:16 + tm, :] = jnp.where(last, 0.0, xn_ref[...])
    cw = cw_ref[...]
    y = cw[0:1] * pad_ref[pl.ds(6, tm), :]
    for j in range(1, DN_CONV):
        y = y + cw[j:j + 1] * pad_ref[pl.ds(6 + j, tm), :]
    y = y * _sigmoid(y)
    gm = gm_ref[...]
    q = y[:, 0:DN_W]
    k = y[:, DN_W:2 * DN_W]
    q_out[...] = q * lax.rsqrt(_group_sum(q * q, gm) + EPS) * (DN_DK ** -0.5)
    k_out[...] = k * lax.rsqrt(_group_sum(k * k, gm) + EPS)
    v_out[...] = y[:, 2 * DN_W:]

    sm = sm_ref[...]
    par = par_ref[...]
    z = sm + par[1:2]
    softplus = jnp.maximum(z, 0.0) + jnp.log1p(jnp.exp(-jnp.abs(z)))
    g = -jnp.exp(par[0:1]) * softplus
    lane = lax.broadcasted_iota(jnp.int32, sm.shape, 1)
    gb_out[...] = jnp.where(lane < 16, g, _sigmoid(sm))


def _dn_prep(proj, conv_w, par, gmat, *, b, s, tm):
    t = b * s
    nb = s // tm
    w3 = 3 * DN_W
    cb = COL_DQKV // w3
    hb = tm // 8
    last_hb = t // 8 - 1
    return pl.pallas_call(
        functools.partial(_dn_prep_kernel, nb=nb, tm=tm),
        grid=(t // tm,),
        in_specs=[pl.BlockSpec((tm, w3), lambda i: (i, cb)),
                  pl.BlockSpec((8, w3), lambda i: (jnp.maximum(i * hb - 1, 0), cb)),
                  pl.BlockSpec((8, w3), lambda i: (jnp.minimum((i + 1) * hb, last_hb), cb)),
                  pl.BlockSpec((tm, 128), lambda i: (i, COL_SMALL // 128)),
                  pl.BlockSpec((8, w3), lambda i: (0, 0)),
                  pl.BlockSpec((8, 128), lambda i: (0, 0)),
                  pl.BlockSpec((DN_W, DN_W), lambda i: (0, 0))],
        out_specs=[pl.BlockSpec((tm, DN_W), lambda i: (i, 0)),
                   pl.BlockSpec((tm, DN_W), lambda i: (i, 0)),
                   pl.BlockSpec((tm, DN_W), lambda i: (i, 0)),
                   pl.BlockSpec((tm, 128), lambda i: (i, 0))],
        out_shape=[jax.ShapeDtypeStruct((t, DN_W), F32)] * 3 + [jax.ShapeDtypeStruct((t, 128), F32)],
        scratch_shapes=[pltpu.VMEM((tm + 16, w3), F32)],
        compiler_params=_cparams(("parallel",)),
        name="dn_prep",
    )(proj, proj, proj, proj, conv_w, par, gmat)


GRP = 256
NGRP = DN_W // GRP


def _bd_rows(x_bf, bmask):
    return jnp.where(bmask, jnp.concatenate([x_bf] * (GRP // DN_CHUNK), axis=0), jnp.zeros((), BF16))


def _dn_chunk_kernel(qf_ref, kf_ref, vf_ref, gf_ref, qb_ref, kb_ref, vb_ref, gb_ref,
                     of_ref, ob_ref, s_ref):
    c = DN_CHUNK

    @pl.when(pl.program_id(1) == 0)
    def _():
        s_ref[...] = jnp.zeros(s_ref.shape, F32)

    ri = lax.broadcasted_iota(jnp.int32, (c, GRP), 0)
    cj = lax.broadcasted_iota(jnp.int32, (c, GRP), 1) & (c - 1)
    br = lax.broadcasted_iota(jnp.int32, (GRP, GRP), 0) >> 6
    bc = lax.broadcasted_iota(jnp.int32, (GRP, GRP), 1) >> 6
    bmask = br == bc
    ident = (ri == cj).astype(F32)

    def pmm(a, bmat):
        return _dot(a.astype(BF16), _bd_rows(bmat.astype(BF16), bmask))

    def pmm_nt(a, bmat):
        return _dot_nt(a.astype(BF16), _bd_rows(bmat.astype(BF16), bmask))

    def inverse(lmat):
        d = jnp.where((ri >> 3) == (cj >> 3), lmat, 0.0)
        d2 = pmm(d, d)
        d4 = pmm(d2, d2)
        x = pmm(ident - d, ident + d2)
        x = pmm(x, ident + d4)
        for sh in (3, 4, 5):
            off = ((ri >> (sh + 1)) == (cj >> (sh + 1))) & ((ri >> sh) != (cj >> sh))
            x = x - pmm(pmm(x, jnp.where(off, lmat, 0.0)), x)
        return x

    er = lax.broadcasted_iota(jnp.int32, (128, DN_W), 0)
    ec = lax.broadcasted_iota(jnp.int32, (128, DN_W), 1) >> 6
    tr = lax.broadcasted_iota(jnp.int32, (c, c), 0)
    tc = lax.broadcasted_iota(jnp.int32, (c, c), 1)
    mr = lax.broadcasted_iota(jnp.int32, (c, DN_W), 0)
    mj = lax.broadcasted_iota(jnp.int32, (c, DN_W), 1) & (c - 1)
    ones_cc = jnp.ones((c, c), BF16)

    for d, (q_ref, k_ref, v_ref, g_ref, o_ref) in enumerate(
            ((qf_ref, kf_ref, vf_ref, gf_ref, of_ref), (qb_ref, kb_ref, vb_ref, gb_ref, ob_ref))):
        q = q_ref[...]
        k = k_ref[...]
        v = v_ref[...]
        g3 = _split3(g_ref[...])
        e_g = (er == d * DN_HEADS + ec).astype(BF16)
        e_b = (er == 16 + d * DN_HEADS + ec).astype(BF16)
        gexp = _dot(g3[0], e_g) + _dot(g3[1], e_g) + _dot(g3[2], e_g)
        bexp = _dot(g3[0], e_b) + _dot(g3[1], e_b) + _dot(g3[2], e_b)
        if d == 0:
            tri = (tc <= tr).astype(BF16)
            mask_t = mr <= mj
            incl = ri >= cj
            strict = ri > cj
        else:
            tri = (tc >= tr).astype(BF16)
            mask_t = mr >= mj
            incl = ri <= cj
            strict = ri < cj
        ge3 = _split3(gexp)
        gcrow = _dot(tri, ge3[0]) + _dot(tri, ge3[1]) + _dot(tri, ge3[2])
        gm3 = _split3(jnp.where(mask_t, gexp, 0.0))
        gccol = _dot(ones_cc, gm3[0]) + _dot(ones_cc, gm3[1]) + _dot(ones_cc, gm3[2])
        for g in range(NGRP):
            sl = slice(g * GRP, (g + 1) * GRP)
            kg = k[:, sl]
            qg = q[:, sl]
            bg = bexp[:, sl]
            gc = gcrow[:, sl]
            dec = jnp.exp(jnp.where(incl, gc - gccol[:, sl], -jnp.inf))
            egc = jnp.exp(gc)
            kk = pmm_nt(kg, kg)
            qk = pmm_nt(qg, kg)
            x = inverse(jnp.where(strict, kk * bg * dec, 0.0))
            u = pmm(x, v[:, sl] * bg)
            w = pmm(x, kg * bg * egc)
            st = s_ref[d * NGRP + g]
            sb = st.astype(BF16)
            vnew = u - _dot(w.astype(BF16), sb)
            att = jnp.where(incl, qk * dec, 0.0)
            o_ref[:, sl] = _dot((qg * egc).astype(BF16), sb) + pmm(att, vnew)
            gl = gc[c - 1:c] if d == 0 else gc[0:1]
            kdec = kg * jnp.exp(gl - gc)
            upd = _dot_tn(kdec.astype(BF16), vnew.astype(BF16))
            s_ref[d * NGRP + g] = st * jnp.exp(gl) + jnp.where(bmask, upd, 0.0)


def _dn_chunk(qn, kn, vv, gb, *, b, s):
    n = s // DN_CHUNK
    t = b * s
    fwd = lambda bi, ci: (bi * n + ci, 0)
    bwd = lambda bi, ci: (bi * n + n - 1 - ci, 0)
    wide = lambda im: pl.BlockSpec((DN_CHUNK, DN_W), im)
    small = lambda im: pl.BlockSpec((DN_CHUNK, 128), im)
    return pl.pallas_call(
        _dn_chunk_kernel,
        grid=(b, n),
        in_specs=[wide(fwd), wide(fwd), wide(fwd), small(fwd), wide(bwd), wide(bwd), wide(bwd), small(bwd)],
        out_specs=[wide(fwd), wide(bwd)],
        out_shape=[jax.ShapeDtypeStruct((t, DN_W), F32)] * 2,
        scratch_shapes=[pltpu.VMEM((2 * NGRP, GRP, GRP), F32)],
        compiler_params=_cparams(("parallel", "arbitrary")),
        name="dn_chunk",
    )(qn, kn, vv, gb, qn, kn, vv, gb)


def _merge_kernel(x_ref, gates_ref, dz_ref, mq_ref, omla_ref, of_ref, ob_ref, mk_ref, mv_ref,
                  onorm_ref, gm_ref, wm_ref, wd_ref, wmem_ref, wout_ref, o_ref):
    o = of_ref[...] + ob_ref[...]
    ms = _group_sum(o * o, gm_ref[...]) * (1.0 / DN_DK)
    dz = dz_ref[...]
    o_dn = o * lax.rsqrt(ms + EPS) * onorm_ref[...] * (dz * _sigmoid(dz))

    mq = mq_ref[...]
    mk = mk_ref[0]
    mv = mv_ref[0]
    outs = []
    for h in range(MEM_HEADS):
        hs = slice(h * MEM_HD, (h + 1) * MEM_HD)
        sc = _dot_nt(mq[:, hs].astype(BF16), mk[:, hs]) * (MEM_HD ** -0.5)
        p = jnp.exp(sc - jnp.max(sc, axis=-1, keepdims=True))
        p = p / jnp.sum(p, axis=-1, keepdims=True)
        outs.append(_dot(p.astype(BF16), mv[:, hs]))
    o_mem = jnp.concatenate(outs, axis=-1)

    gates = gates_ref[...]
    merged = (_sigmoid(gates[:, 0:D_MODEL]) * _dot(omla_ref[...], wm_ref[...])
              + _sigmoid(gates[:, D_MODEL:2 * D_MODEL]) * _dot(o_dn.astype(BF16), wd_ref[...])
              + _sigmoid(gates[:, 2 * D_MODEL:]) * _dot(o_mem.astype(BF16), wmem_ref[...]))
    o_ref[...] = x_ref[...] + _dot(merged.astype(BF16), wout_ref[...])


def _merge(x, proj, omla, o_f, o_b, memkv, onorm, gmat, wm, wd, wmem, wout, *, b, s, tm):
    t = b * s
    nb = s // tm
    mt = memkv.shape[1]
    const = lambda shape: pl.BlockSpec(shape, lambda i: (0,) * len(shape))
    return pl.pallas_call(
        _merge_kernel,
        grid=(t // tm,),
        in_specs=[pl.BlockSpec((tm, D_MODEL), lambda i: (i, 0)),
                  pl.BlockSpec((tm, 3 * D_MODEL), lambda i: (i, COL_GATES // (3 * D_MODEL))),
                  pl.BlockSpec((tm, DN_W), lambda i: (i, COL_DZ // DN_W)),
                  pl.BlockSpec((tm, MEM_W), lambda i: (i, COL_MQ // MEM_W)),
                  pl.BlockSpec((tm, DN_W), lambda i: (i, 0)),
                  pl.BlockSpec((tm, DN_W), lambda i: (i, 0)),
                  pl.BlockSpec((tm, DN_W), lambda i: (i, 0)),
                  pl.BlockSpec((1, mt, MEM_W), lambda i: (i // nb, 0, 0)),
                  pl.BlockSpec((1, mt, MEM_W), lambda i: (i // nb, 0, 1)),
                  const((1, DN_W)), const((DN_W, DN_W)),
                  const((DN_W, D_MODEL)), const((DN_W, D_MODEL)), const((MEM_W, D_MODEL)),
                  const((D_MODEL, D_MODEL))],
        out_specs=pl.BlockSpec((tm, D_MODEL), lambda i: (i, 0)),
        out_shape=jax.ShapeDtypeStruct((t, D_MODEL), F32),
        compiler_params=_cparams(("parallel",)),
        name="merge",
    )(x, proj, proj, proj, omla, o_f, o_b, memkv, memkv, onorm, gmat, wm, wd, wmem, wout)


def _ffn_kernel(x_ref, nw_ref, wg_ref, wu_ref, wd_ref, fw_ref, o_ref, h_ref, acc_ref, *, nf, final_norm):
    j = pl.program_id(1)

    @pl.when(j == 0)
    def _():
        h_ref[...] = _rms(x_ref[...], nw_ref[...]).astype(BF16)
        acc_ref[...] = jnp.zeros(acc_ref.shape, F32)

    h = h_ref[...]
    gate = _dot(h, wg_ref[...])
    up = _dot(h, wu_ref[...])
    act = (gate * _sigmoid(gate) * up).astype(BF16)
    acc_ref[...] += _dot(act, wd_ref[...])

    @pl.when(j == nf - 1)
    def _():
        y = x_ref[...] + acc_ref[...]
        if final_norm:
            y = _rms(y, fw_ref[...])
        o_ref[...] = y


def _ffn(x, nw, wgu, wd, fw, *, tm, tf, final_norm):
    t = x.shape[0]
    nf = D_FF // tf
    return pl.pallas_call(
        functools.partial(_ffn_kernel, nf=nf, final_norm=final_norm),
        grid=(t // tm, nf),
        in_specs=[pl.BlockSpec((tm, D_MODEL), lambda i, j: (i, 0)),
                  pl.BlockSpec((1, D_MODEL), lambda i, j: (0, 0)),
                  pl.BlockSpec((D_MODEL, tf), lambda i, j: (0, j)),
                  pl.BlockSpec((D_MODEL, tf), lambda i, j: (0, j + nf)),
                  pl.BlockSpec((tf, D_MODEL), lambda i, j: (j, 0)),
                  pl.BlockSpec((1, D_MODEL), lambda i, j: (0, 0))],
        out_specs=pl.BlockSpec((tm, D_MODEL), lambda i, j: (i, 0)),
        out_shape=jax.ShapeDtypeStruct((t, D_MODEL), F32),
        scratch_shapes=[pltpu.VMEM((tm, D_MODEL), BF16), pltpu.VMEM((tm, D_MODEL), F32)],
        compiler_params=_cparams(("parallel", "arbitrary")),
        name="ffn",
    )(x, nw, wgu, wgu, wd, fw)


def _pick(n, pref):
    return pref if n % pref == 0 else n


def _rope_tables(positions):
    b, s = positions.shape
    inv_freq = 1.0 / (ROPE_THETA ** (jnp.arange(0, MLA_ROPE, 2, dtype=F32) / MLA_ROPE))
    ang = positions.astype(F32)[..., None] * inv_freq
    cos = jnp.cos(ang).reshape(b * s, MLA_ROPE // 2)
    sin = jnp.sin(ang).reshape(b * s, MLA_ROPE // 2)
    z64 = jnp.zeros((b * s, MLA_NOPE), F32)
    z32 = jnp.zeros((b * s, 32), F32)
    ck = jnp.concatenate([z64, cos, cos, z32], axis=1)
    sk = jnp.concatenate([z64, sin, sin, z32], axis=1)
    qscale = (MLA_QK ** -0.5) * LOG2E
    cq = jnp.concatenate([jnp.ones_like(z64), cos, cos, z32], axis=1) * qscale
    sq = sk * qscale
    to_t = lambda a: a.reshape(b, s, 128).transpose(0, 2, 1)
    return ck, sk, to_t(cq), to_t(sq)


def _layer_weights(w_in, w_uq, w_ukv):
    d = w_in.shape[0]
    z = lambda n: jnp.zeros((d, n), w_in.dtype)
    kr = w_in[:, 512:544]
    w_re = jnp.concatenate([
        w_in[:, 3136:6208], w_in[:, 544:2080], w_in[:, 2080:2592], w_in[:, 2624:3136],
        w_in[:, 0:256], w_in[:, 256:512],
        w_in[:, 2592:2624], z(32), kr, z(32),
        z(64), -kr[:, 16:32], kr[:, 0:16], z(32)], axis=1).astype(BF16)

    uq = w_uq.reshape(MLA_Q_LORA, MLA_HEADS, MLA_QK)
    zq = lambda n: jnp.zeros((MLA_Q_LORA, MLA_HEADS, n), w_uq.dtype)
    x1, x2 = uq[..., 64:80], uq[..., 80:96]
    wq1 = jnp.concatenate([uq[..., :64], x1, x2, zq(32)], axis=-1).reshape(MLA_Q_LORA, 1024)
    wq2 = jnp.concatenate([zq(64), -x2, x1, zq(32)], axis=-1).reshape(MLA_Q_LORA, 1024)
    ukv = w_ukv.reshape(MLA_KV_LORA, MLA_HEADS, MLA_NOPE + MLA_V)
    wk = jnp.concatenate([ukv[..., :64], jnp.zeros((MLA_KV_LORA, MLA_HEADS, 64), w_ukv.dtype)],
                         axis=-1).reshape(MLA_KV_LORA, 1024)
    wv = ukv[..., 64:].reshape(MLA_KV_LORA, MLA_HEADS * MLA_V)
    return w_re, wq1.T.astype(BF16), wq2.T.astype(BF16), wk.astype(BF16), wv.T.astype(BF16)


def kernel(x, mem, positions, norm_mix, w_in, mla_q_norm, mla_w_uq, mla_kv_norm, mla_w_ukv, dn_conv, dn_a_log, dn_dt_bias, dn_out_norm, mem_norm, mem_w_kv, w_branch_mla, w_branch_dn, w_branch_mem, w_out, norm_ffn, ffn_w_gate_up, ffn_w_down, final_norm):
    b, s, d = x.shape
    t = b * s
    depth = w_in.shape[0]
    mt = mem.shape[1]
    ck, sk, cqt, sqt = _rope_tables(positions)
    gi = jnp.arange(DN_W) // DN_DK
    gmat = (gi[:, None] == gi[None, :]).astype(BF16)
    xf = x.reshape(t, d)
    memf = mem.reshape(b * mt, d)
    tm = _pick(s, 512)
    for l in range(depth):
        w_re, wq1t, wq2t, wk, wvt = _layer_weights(w_in[l], mla_w_uq[l], mla_w_ukv[l])
        proj = _norm_matmul(xf, norm_mix[l][None], w_re, tm=_pick(t, 1024), tn=640, out_dtype=F32)

        qt, kk, vt = _mla_prep(proj, ck, sk, cqt, sqt, mla_q_norm[l][None], mla_kv_norm[l][None],
                               wq1t, wq2t, wk, wvt, b=b, s=s, tm=tm)
        o_mla = _flash(qt, kk, vt, b=b, s=s, tq=_pick(s, 512), tk=_pick(s, 256))

        conv_w = jnp.concatenate([dn_conv[l], jnp.zeros((8 - DN_CONV, 3 * DN_W), F32)], axis=0)
        pad16 = lambda v: jnp.concatenate([v.reshape(-1), jnp.zeros((128 - 2 * DN_HEADS,), F32)])
        par = jnp.concatenate([pad16(dn_a_log[l])[None], pad16(dn_dt_bias[l])[None],
                               jnp.zeros((6, 128), F32)], axis=0)
        qn, kn, vv, gb = _dn_prep(proj, conv_w, par, gmat, b=b, s=s, tm=tm)
        o_f, o_b = _dn_chunk(qn, kn, vv, gb, b=b, s=s)
        n_chunks = s // DN_CHUNK
        perm = lambda a: a.reshape(b, n_chunks, DN_CHUNK, DN_W).swapaxes(1, 2).reshape(t, DN_W)
        o_f, o_b = perm(o_f), perm(o_b)

        memkv = _norm_matmul(memf, mem_norm[l][None], mem_w_kv[l].astype(BF16),
                             tm=b * mt, tn=512, out_dtype=BF16).reshape(b, mt, 2 * MEM_W)
        onorm = jnp.tile(dn_out_norm[l], DN_HEADS)[None]
        xf = _merge(xf, proj, o_mla, o_f, o_b, memkv, onorm, gmat,
                    w_branch_mla[l].astype(BF16), w_branch_dn[l].astype(BF16),
                    w_branch_mem[l].astype(BF16), w_out[l].astype(BF16), b=b, s=s, tm=tm)
        xf = _ffn(xf, norm_ffn[l][None], ffn_w_gate_up[l].astype(BF16), ffn_w_down[l].astype(BF16),
                  final_norm[None], tm=tm, tf=1408, final_norm=(l == depth - 1))
    return xf.reshape(b, s, d)
```

```python
import functools

import jax
import jax.numpy as jnp
from jax import lax
from jax.experimental import pallas as pl
from jax.experimental.pallas import tpu as pltpu

F32 = jnp.float32
BF16 = jnp.bfloat16

D_MODEL = 1024
EPS = 1e-6
MLA_HEADS = 8
MLA_Q_LORA = 256
MLA_KV_LORA = 256
MLA_NOPE = 64
MLA_ROPE = 32
MLA_V = 64
MLA_QK = MLA_NOPE + MLA_ROPE
ROPE_THETA = 10000.0
DN_HEADS = 8
DN_DK = 64
DN_W = DN_HEADS * DN_DK
DN_CONV = 5
DN_CHUNK = 64
MEM_HEADS = 4
MEM_HD = 128
MEM_W = MEM_HEADS * MEM_HD
D_FF = 2816

COL_GATES = 0
COL_DQKV = 3072
COL_DZ = 4608
COL_MQ = 5120
COL_CQ = 5632
COL_CKV = 5888
COL_SMALL = 6144
COL_SMALL2 = 6272
IN_COLS_P = 6400

HEAD_SLAB = 128
VT_ROWS = 80
LOG2E = 1.4426950408889634
VMEM_LIMIT = 48 * 1024 * 1024


def _cparams(sem):
    return pltpu.CompilerParams(dimension_semantics=sem, vmem_limit_bytes=VMEM_LIMIT)


def _sigmoid(x):
    return 1.0 / (1.0 + jnp.exp(-x))


def _dot(a, b):
    return jnp.dot(a, b, preferred_element_type=F32)


def _dot_nt(a, b):
    return lax.dot_general(a, b, (((1,), (1,)), ((), ())), preferred_element_type=F32)


def _dot_tn(a, b):
    return lax.dot_general(a, b, (((0,), (0,)), ((), ())), preferred_element_type=F32)


def _split3(x):
    x1 = x.astype(BF16)
    r1 = x - x1.astype(F32)
    x2 = r1.astype(BF16)
    x3 = (r1 - x2.astype(F32)).astype(BF16)
    return x1, x2, x3


def _rms(x, w):
    return x * lax.rsqrt(jnp.mean(x * x, axis=-1, keepdims=True) + EPS) * w


def _norm_matmul_kernel(x_ref, g_ref, w_ref, o_ref, h_ref):
    @pl.when(pl.program_id(1) == 0)
    def _():
        h_ref[...] = _rms(x_ref[...], g_ref[...]).astype(BF16)

    o_ref[...] = _dot(h_ref[...], w_ref[...]).astype(o_ref.dtype)


def _norm_matmul(x, g, w, *, tm, tn, out_dtype):
    t, d = x.shape
    n = w.shape[1]
    return pl.pallas_call(
        _norm_matmul_kernel,
        grid=(t // tm, n // tn),
        in_specs=[pl.BlockSpec((tm, d), lambda i, j: (i, 0)),
                  pl.BlockSpec((1, d), lambda i, j: (0, 0)),
                  pl.BlockSpec((d, tn), lambda i, j: (0, j))],
        out_specs=pl.BlockSpec((tm, tn), lambda i, j: (i, j)),
        out_shape=jax.ShapeDtypeStruct((t, n), out_dtype),
        scratch_shapes=[pltpu.VMEM((tm, d), BF16)],
        compiler_params=_cparams(("parallel", "arbitrary")),
        name="norm_matmul",
    )(x, g, w)


def _mla_prep_kernel(cq_ref, ckv_ref, sm_ref, ck_ref, sk_ref, cqt_ref, sqt_ref, qn_ref, kvn_ref,
                     wq1t_ref, wq2t_ref, wk_ref, wvt_ref, qt_out, k_out, vt_out):
    qn = _rms(cq_ref[...], qn_ref[...]).astype(BF16)
    q1t = _dot_nt(wq1t_ref[...], qn)
    q2t = _dot_nt(wq2t_ref[...], qn)
    ct = cqt_ref[0]
    st = sqt_ref[0]
    for h in range(MLA_HEADS):
        hs = slice(h * HEAD_SLAB, (h + 1) * HEAD_SLAB)
        qt_out[0, hs, :] = (q1t[hs] * ct + q2t[hs] * st).astype(BF16)

    kvn = _rms(ckv_ref[...], kvn_ref[...]).astype(BF16)
    kk = _dot(kvn, wk_ref[...])
    sm = sm_ref[...]
    kr = sm[:, :128] * ck_ref[...] + sm[:, 128:] * sk_ref[...]
    for h in range(MLA_HEADS):
        hs = slice(h * HEAD_SLAB, (h + 1) * HEAD_SLAB)
        k_out[:, hs] = (kk[:, hs] + kr).astype(BF16)

    vt = _dot_nt(wvt_ref[...], kvn)
    tm = vt.shape[1]
    ones = jnp.ones((VT_ROWS - MLA_V, tm), BF16)
    for h in range(MLA_HEADS):
        vt_out[0, h * VT_ROWS:h * VT_ROWS + MLA_V, :] = vt[h * MLA_V:(h + 1) * MLA_V].astype(BF16)
        vt_out[0, h * VT_ROWS + MLA_V:(h + 1) * VT_ROWS, :] = ones


def _mla_prep(proj, ck, sk, cqt, sqt, qnorm, kvnorm, wq1t, wq2t, wk, wvt, *, b, s, tm):
    nb = s // tm
    t = b * s
    row = lambda bi, i: bi * nb + i
    const = lambda shape: pl.BlockSpec(shape, lambda bi, i: (0,) * len(shape))
    return pl.pallas_call(
        _mla_prep_kernel,
        grid=(b, nb),
        in_specs=[pl.BlockSpec((tm, 256), lambda bi, i: (row(bi, i), COL_CQ // 256)),
                  pl.BlockSpec((tm, 256), lambda bi, i: (row(bi, i), COL_CKV // 256)),
                  pl.BlockSpec((tm, 256), lambda bi, i: (row(bi, i), COL_SMALL // 256)),
                  pl.BlockSpec((tm, 128), lambda bi, i: (row(bi, i), 0)),
                  pl.BlockSpec((tm, 128), lambda bi, i: (row(bi, i), 0)),
                  pl.BlockSpec((1, 128, tm), lambda bi, i: (bi, 0, i)),
                  pl.BlockSpec((1, 128, tm), lambda bi, i: (bi, 0, i)),
                  const((1, 256)), const((1, 256)),
                  const((1024, 256)), const((1024, 256)), const((256, 1024)), const((512, 256))],
        out_specs=[pl.BlockSpec((1, 1024, tm), lambda bi, i: (bi, 0, i)),
                   pl.BlockSpec((tm, 1024), lambda bi, i: (row(bi, i), 0)),
                   pl.BlockSpec((1, MLA_HEADS * VT_ROWS, tm), lambda bi, i: (bi, 0, i))],
        out_shape=[jax.ShapeDtypeStruct((b, 1024, s), BF16),
                   jax.ShapeDtypeStruct((t, 1024), BF16),
                   jax.ShapeDtypeStruct((b, MLA_HEADS * VT_ROWS, s), BF16)],
        compiler_params=_cparams(("parallel", "parallel")),
        name="mla_prep",
    )(proj, proj, proj, ck, sk, cqt, sqt, qnorm, kvnorm, wq1t, wq2t, wk, wvt)


def _flash_kernel(qt_ref, k_ref, vt_ref, o_ref, m_ref, acc_ref, s_buf, p_buf, a_buf, *, tk, nk):
    m_ref[...] = jnp.full(m_ref.shape, -jnp.inf, F32)
    acc_ref[...] = jnp.zeros(acc_ref.shape, F32)

    def scores(j, par):
        r0 = pl.multiple_of(j * tk, tk)
        kc = k_ref[pl.ds(r0, tk), :]
        for h in range(2):
            qt = qt_ref[0, h * HEAD_SLAB:(h + 1) * HEAD_SLAB, :]
            s_buf[par, h] = _dot(kc[:, h * HEAD_SLAB:(h + 1) * HEAD_SLAB], qt)

    def softmax(par):
        for h in range(2):
            st = s_buf[par, h]
            m_old = m_ref[h]
            m_new = jnp.maximum(m_old, jnp.max(st, axis=0, keepdims=True))
            a_buf[par, h] = jnp.exp2(m_old - m_new)
            p_buf[par, h] = jnp.exp2(st - m_new).astype(BF16)
            m_ref[h] = m_new

    def pv(j, par):
        r0 = pl.multiple_of(j * tk, tk)
        vc = vt_ref[0, :, pl.ds(r0, tk)]
        for h in range(2):
            acc_ref[h] = a_buf[par, h] * acc_ref[h] + _dot(vc[h * VT_ROWS:(h + 1) * VT_ROWS, :], p_buf[par, h])

    scores(0, 0)
    scores(1, 1)
    softmax(0)

    def body(i, carry):
        j = 2 * i
        pv(j, 0)
        softmax(1)
        scores(j + 2, 0)
        pv(j + 1, 1)
        softmax(0)
        scores(j + 3, 1)
        return carry

    lax.fori_loop(0, nk // 2 - 1, body, 0)
    pv(nk - 2, 0)
    softmax(1)
    pv(nk - 1, 1)

    outs = []
    for h in range(2):
        a = acc_ref[h]
        outs.append(a[0:MLA_V] / a[MLA_V:MLA_V + 1])
    ot = jnp.concatenate(outs, axis=0)
    o_ref[...] = ot.T.astype(o_ref.dtype)


def _flash(qt, k, vt, *, b, s, tq, tk):
    nq = s // tq
    nk = s // tk
    hp = MLA_HEADS // 2
    return pl.pallas_call(
        functools.partial(_flash_kernel, tk=tk, nk=nk),
        grid=(b, hp, nq),
        in_specs=[pl.BlockSpec((1, 2 * HEAD_SLAB, tq), lambda bi, p, qi: (bi, p, qi)),
                  pl.BlockSpec((s, 2 * HEAD_SLAB), lambda bi, p, qi: (bi, p)),
                  pl.BlockSpec((1, 2 * VT_ROWS, s), lambda bi, p, qi: (bi, p, 0))],
        out_specs=pl.BlockSpec((tq, 2 * MLA_V), lambda bi, p, qi: (bi * nq + qi, p)),
        out_shape=jax.ShapeDtypeStruct((b * s, MLA_HEADS * MLA_V), BF16),
        scratch_shapes=[pltpu.VMEM((2, 1, tq), F32), pltpu.VMEM((2, VT_ROWS, tq), F32),
                        pltpu.VMEM((2, 2, tk, tq), F32), pltpu.VMEM((2, 2, tk, tq), BF16),
                        pltpu.VMEM((2, 2, 1, tq), F32)],
        compiler_params=_cparams(("parallel", "parallel", "arbitrary")),
        name="mla_flash",
    )(qt, k, vt)


def _group_sum(z, gmat):
    z1 = z.astype(BF16)
    z2 = (z - z1.astype(F32)).astype(BF16)
    return _dot(z1, gmat) + _dot(z2, gmat)


def _dn_prep_kernel(x_ref, xp_ref, xn_ref, sm_ref, cw_ref, par_ref, gm_ref,
                    q_out, k_out, v_out, gb_out, pad_ref, *, nb, tm):
    i = pl.program_id(0)
    first = (i % nb) == 0
    last = (i % nb) == nb - 1
    pad_ref[0:8, :] = jnp.where(first, 0.0, xp_ref[...])
    pad_ref[8:8 + tm, :] = x_ref[...]
    pad_ref[8 + tm:16 + tm, :] = jnp.where(last, 0.0, xn_ref[...])
    cw = cw_ref[...]
    y = cw[0:1] * pad_ref[pl.ds(6, tm), :]
    for j in range(1, DN_CONV):
        y = y + cw[j:j + 1] * pad_ref[pl.ds(6 + j, tm), :]
    y = y * _sigmoid(y)
    gm = gm_ref[...]
    q = y[:, 0:DN_W]
    k = y[:, DN_W:2 * DN_W]
    q_out[...] = q * lax.rsqrt(_group_sum(q * q, gm) + EPS) * (DN_DK ** -0.5)
    k_out[...] = k * lax.rsqrt(_group_sum(k * k, gm) + EPS)
    v_out[...] = y[:, 2 * DN_W:]

    sm = sm_ref[...]
    par = par_ref[...]
    z = sm + par[1:2]
    softplus = jnp.maximum(z, 0.0) + jnp.log1p(jnp.exp(-jnp.abs(z)))
    g = -jnp.exp(par[0:1]) * softplus
    lane = lax.broadcasted_iota(jnp.int32, sm.shape, 1)
    gb_out[...] = jnp.where(lane < 16, g, _sigmoid(sm))


def _dn_prep(proj, conv_w, par, gmat, *, b, s, tm):
    t = b * s
    nb = s // tm
    w3 = 3 * DN_W
    cb = COL_DQKV // w3
    hb = tm // 8
    last_hb = t // 8 - 1
    return pl.pallas_call(
        functools.partial(_dn_prep_kernel, nb=nb, tm=tm),
        grid=(t // tm,),
        in_specs=[pl.BlockSpec((tm, w3), lambda i: (i, cb)),
                  pl.BlockSpec((8, w3), lambda i: (jnp.maximum(i * hb - 1, 0), cb)),
                  pl.BlockSpec((8, w3), lambda i: (jnp.minimum((i + 1) * hb, last_hb), cb)),
                  pl.BlockSpec((tm, 128), lambda i: (i, COL_SMALL // 128)),
                  pl.BlockSpec((8, w3), lambda i: (0, 0)),
                  pl.BlockSpec((8, 128), lambda i: (0, 0)),
                  pl.BlockSpec((DN_W, DN_W), lambda i: (0, 0))],
        out_specs=[pl.BlockSpec((tm, DN_W), lambda i: (i, 0)),
                   pl.BlockSpec((tm, DN_W), lambda i: (i, 0)),
                   pl.BlockSpec((tm, DN_W), lambda i: (i, 0)),
                   pl.BlockSpec((tm, 128), lambda i: (i, 0))],
        out_shape=[jax.ShapeDtypeStruct((t, DN_W), F32)] * 3 + [jax.ShapeDtypeStruct((t, 128), F32)],
        scratch_shapes=[pltpu.VMEM((tm + 16, w3), F32)],
        compiler_params=_cparams(("parallel",)),
        name="dn_prep",
    )(proj, proj, proj, proj, conv_w, par, gmat)


GRP = 256
NGRP = DN_W // GRP


def _bd_rows(x_bf, bmask):
    return jnp.where(bmask, jnp.concatenate([x_bf] * (GRP // DN_CHUNK), axis=0), jnp.zeros((), BF16))


def _dn_chunk_kernel(qf_ref, kf_ref, vf_ref, gf_ref, qb_ref, kb_ref, vb_ref, gb_ref,
                     of_ref, ob_ref, s_ref, *, nch):
    c = DN_CHUNK

    @pl.when(pl.program_id(1) == 0)
    def _():
        s_ref[...] = jnp.zeros(s_ref.shape, F32)

    ri = lax.broadcasted_iota(jnp.int32, (c, GRP), 0)
    cj = lax.broadcasted_iota(jnp.int32, (c, GRP), 1) & (c - 1)
    br = lax.broadcasted_iota(jnp.int32, (GRP, GRP), 0) >> 6
    bc = lax.broadcasted_iota(jnp.int32, (GRP, GRP), 1) >> 6
    bmask = br == bc
    ident = (ri == cj).astype(F32)
    incl = (ri >= cj, ri <= cj)
    strict = (ri > cj, ri < cj)

    def pmm(a, bmat):
        return _dot(a.astype(BF16), _bd_rows(bmat.astype(BF16), bmask))

    def pmm_nt(a, bmat):
        return _dot_nt(a.astype(BF16), _bd_rows(bmat.astype(BF16), bmask))

    er = lax.broadcasted_iota(jnp.int32, (128, DN_W), 0)
    ec = lax.broadcasted_iota(jnp.int32, (128, DN_W), 1) >> 6
    tr = lax.broadcasted_iota(jnp.int32, (c, c), 0)
    tc = lax.broadcasted_iota(jnp.int32, (c, c), 1)
    mr = lax.broadcasted_iota(jnp.int32, (c, DN_W), 0)
    mj = lax.broadcasted_iota(jnp.int32, (c, DN_W), 1) & (c - 1)
    ones_cc = jnp.ones((c, c), BF16)
    tri = ((tc <= tr).astype(BF16), (tc >= tr).astype(BF16))
    mask_t = (mr <= mj, mr >= mj)
    refs = ((qf_ref, kf_ref, vf_ref, gf_ref, of_ref), (qb_ref, kb_ref, vb_ref, gb_ref, ob_ref))

    def rows(d, step):
        i = step if d == 0 else nch - 1 - step
        return slice(i * c, (i + 1) * c)

    def sum3(lhs, rhs3):
        return _dot(lhs, rhs3[0]) + _dot(lhs, rhs3[1]) + _dot(lhs, rhs3[2])

    ds = [(d, step) for step in range(nch) for d in (0, 1)]
    g3 = {u: _split3(refs[u[0]][3][rows(*u), :]) for u in ds}
    e_g = [(er == d * DN_HEADS + ec).astype(BF16) for d in (0, 1)]
    e_b = [(er == 16 + d * DN_HEADS + ec).astype(BF16) for d in (0, 1)]
    gexp = {u: _dot(g3[u][0], e_g[u[0]]) + _dot(g3[u][1], e_g[u[0]]) + _dot(g3[u][2], e_g[u[0]]) for u in ds}
    bexp = {u: _dot(g3[u][0], e_b[u[0]]) + _dot(g3[u][1], e_b[u[0]]) + _dot(g3[u][2], e_b[u[0]]) for u in ds}
    ge3 = {u: _split3(gexp[u]) for u in ds}
    gm3 = {u: _split3(jnp.where(mask_t[u[0]], gexp[u], 0.0)) for u in ds}
    gcrow = {u: sum3(tri[u[0]], ge3[u]) for u in ds}
    gccol = {u: sum3(ones_cc, gm3[u]) for u in ds}

    units = [(d, step, g) for step in range(nch) for d in (0, 1) for g in range(NGRP)]
    sl = lambda g: slice(g * GRP, (g + 1) * GRP)
    kg = {u: refs[u[0]][1][rows(u[0], u[1]), sl(u[2])] for u in units}
    qg = {u: refs[u[0]][0][rows(u[0], u[1]), sl(u[2])] for u in units}
    vg = {u: refs[u[0]][2][rows(u[0], u[1]), sl(u[2])] for u in units}
    bg = {u: bexp[u[:2]][:, sl(u[2])] for u in units}
    gc = {u: gcrow[u[:2]][:, sl(u[2])] for u in units}
    dec = {u: jnp.exp(jnp.where(incl[u[0]], gc[u] - gccol[u[:2]][:, sl(u[2])], -jnp.inf)) for u in units}
    egc = {u: jnp.exp(gc[u]) for u in units}
    kk = {u: pmm_nt(kg[u], kg[u]) for u in units}
    qk = {u: pmm_nt(qg[u], kg[u]) for u in units}
    lm = {u: jnp.where(strict[u[0]], kk[u] * bg[u] * dec[u], 0.0) for u in units}

    dd = {u: jnp.where((ri >> 3) == (cj >> 3), lm[u], 0.0) for u in units}
    d2 = {u: pmm(dd[u], dd[u]) for u in units}
    d4 = {u: pmm(d2[u], d2[u]) for u in units}
    x = {u: pmm(ident - dd[u], ident + d2[u]) for u in units}
    x = {u: pmm(x[u], ident + d4[u]) for u in units}
    for sh in (3, 4, 5):
        off = ((ri >> (sh + 1)) == (cj >> (sh + 1))) & ((ri >> sh) != (cj >> sh))
        xm = {u: pmm(x[u], jnp.where(off, lm[u], 0.0)) for u in units}
        x = {u: x[u] - pmm(xm[u], x[u]) for u in units}
    uu = {u: pmm(x[u], vg[u] * bg[u]) for u in units}
    ww = {u: pmm(x[u], kg[u] * bg[u] * egc[u]).astype(BF16) for u in units}
    att = {u: jnp.where(incl[u[0]], qk[u] * dec[u], 0.0).astype(BF16) for u in units}
    qe = {u: (qg[u] * egc[u]).astype(BF16) for u in units}
    gl = {u: (gc[u][c - 1:c] if u[0] == 0 else gc[u][0:1]) for u in units}
    kdec = {u: (kg[u] * jnp.exp(gl[u] - gc[u])).astype(BF16) for u in units}
    egl = {u: jnp.exp(gl[u]) for u in units}

    for step in range(nch):
        cur = [(d, step, g) for d in (0, 1) for g in range(NGRP)]
        st = {u: s_ref[u[0] * NGRP + u[2]] for u in cur}
        sb = {u: st[u].astype(BF16) for u in cur}
        vnew = {u: uu[u] - _dot(ww[u], sb[u]) for u in cur}
        vbd = {u: _bd_rows(vnew[u].astype(BF16), bmask) for u in cur}
        for u in cur:
            refs[u[0]][4][rows(u[0], step), sl(u[2])] = _dot(qe[u], sb[u]) + _dot(att[u], vbd[u])
        upd = {u: _dot_tn(kdec[u], vnew[u].astype(BF16)) for u in cur}
        for u in cur:
            s_ref[u[0] * NGRP + u[2]] = st[u] * egl[u] + jnp.where(bmask, upd[u], 0.0)


def _dn_chunk(qn, kn, vv, gb, *, b, s, nch):
    n = s // (DN_CHUNK * nch)
    t = b * s
    fwd = lambda bi, ci: (bi * n + ci, 0)
    bwd = lambda bi, ci: (bi * n + n - 1 - ci, 0)
    wide = lambda im: pl.BlockSpec((DN_CHUNK * nch, DN_W), im)
    small = lambda im: pl.BlockSpec((DN_CHUNK * nch, 128), im)
    return pl.pallas_call(
        functools.partial(_dn_chunk_kernel, nch=nch),
        grid=(b, n),
        in_specs=[wide(fwd), wide(fwd), wide(fwd), small(fwd), wide(bwd), wide(bwd), wide(bwd), small(bwd)],
        out_specs=[wide(fwd), wide(bwd)],
        out_shape=[jax.ShapeDtypeStruct((t, DN_W), F32)] * 2,
        scratch_shapes=[pltpu.VMEM((2 * NGRP, GRP, GRP), F32)],
        compiler_params=_cparams(("parallel", "arbitrary")),
        name="dn_chunk",
    )(qn, kn, vv, gb, qn, kn, vv, gb)


def _merge_kernel(x_ref, gates_ref, dz_ref, mq_ref, omla_ref, of_ref, ob_ref, mk_ref, mv_ref,
                  onorm_ref, gm_ref, wm_ref, wd_ref, wmem_ref, wout_ref, o_ref):
    o = of_ref[...] + ob_ref[...]
    ms = _group_sum(o * o, gm_ref[...]) * (1.0 / DN_DK)
    dz = dz_ref[...]
    o_dn = o * lax.rsqrt(ms + EPS) * onorm_ref[...] * (dz * _sigmoid(dz))

    mq = mq_ref[...]
    mk = mk_ref[0]
    mv = mv_ref[0]
    outs = []
    for h in range(MEM_HEADS):
        hs = slice(h * MEM_HD, (h + 1) * MEM_HD)
        sc = _dot_nt(mq[:, hs].astype(BF16), mk[:, hs]) * (MEM_HD ** -0.5)
        p = jnp.exp(sc - jnp.max(sc, axis=-1, keepdims=True))
        p = p / jnp.sum(p, axis=-1, keepdims=True)
        outs.append(_dot(p.astype(BF16), mv[:, hs]))
    o_mem = jnp.concatenate(outs, axis=-1)

    gates = gates_ref[...]
    merged = (_sigmoid(gates[:, 0:D_MODEL]) * _dot(omla_ref[...], wm_ref[...])
              + _sigmoid(gates[:, D_MODEL:2 * D_MODEL]) * _dot(o_dn.astype(BF16), wd_ref[...])
              + _sigmoid(gates[:, 2 * D_MODEL:]) * _dot(o_mem.astype(BF16), wmem_ref[...]))
    o_ref[...] = x_ref[...] + _dot(merged.astype(BF16), wout_ref[...])


def _merge(x, proj, omla, o_f, o_b, memkv, onorm, gmat, wm, wd, wmem, wout, *, b, s, tm):
    t = b * s
    nb = s // tm
    mt = memkv.shape[1]
    const = lambda shape: pl.BlockSpec(shape, lambda i: (0,) * len(shape))
    return pl.pallas_call(
        _merge_kernel,
        grid=(t // tm,),
        in_specs=[pl.BlockSpec((tm, D_MODEL), lambda i: (i, 0)),
                  pl.BlockSpec((tm, 3 * D_MODEL), lambda i: (i, COL_GATES // (3 * D_MODEL))),
                  pl.BlockSpec((tm, DN_W), lambda i: (i, COL_DZ // DN_W)),
                  pl.BlockSpec((tm, MEM_W), lambda i: (i, COL_MQ // MEM_W)),
                  pl.BlockSpec((tm, DN_W), lambda i: (i, 0)),
                  pl.BlockSpec((tm, DN_W), lambda i: (i, 0)),
                  pl.BlockSpec((tm, DN_W), lambda i: (i, 0)),
                  pl.BlockSpec((1, mt, MEM_W), lambda i: (i // nb, 0, 0)),
                  pl.BlockSpec((1, mt, MEM_W), lambda i: (i // nb, 0, 1)),
                  const((1, DN_W)), const((DN_W, DN_W)),
                  const((DN_W, D_MODEL)), const((DN_W, D_MODEL)), const((MEM_W, D_MODEL)),
                  const((D_MODEL, D_MODEL))],
        out_specs=pl.BlockSpec((tm, D_MODEL), lambda i: (i, 0)),
        out_shape=jax.ShapeDtypeStruct((t, D_MODEL), F32),
        compiler_params=_cparams(("parallel",)),
        name="merge",
    )(x, proj, proj, proj, omla, o_f, o_b, memkv, memkv, onorm, gmat, wm, wd, wmem, wout)


def _ffn_kernel(x_ref, nw_ref, wg_ref, wu_ref, wd_ref, fw_ref, o_ref, h_ref, acc_ref, *, nf, final_norm):
    j = pl.program_id(1)

    @pl.when(j == 0)
    def _():
        h_ref[...] = _rms(x_ref[...], nw_ref[...]).astype(BF16)
        acc_ref[...] = jnp.zeros(acc_ref.shape, F32)

    h = h_ref[...]
    gate = _dot(h, wg_ref[...])
    up = _dot(h, wu_ref[...])
    act = (gate * _sigmoid(gate) * up).astype(BF16)
    acc_ref[...] += _dot(act, wd_ref[...])

    @pl.when(j == nf - 1)
    def _():
        y = x_ref[...] + acc_ref[...]
        if final_norm:
            y = _rms(y, fw_ref[...])
        o_ref[...] = y


def _ffn(x, nw, wgu, wd, fw, *, tm, tf, final_norm):
    t = x.shape[0]
    nf = D_FF // tf
    return pl.pallas_call(
        functools.partial(_ffn_kernel, nf=nf, final_norm=final_norm),
        grid=(t // tm, nf),
        in_specs=[pl.BlockSpec((tm, D_MODEL), lambda i, j: (i, 0)),
                  pl.BlockSpec((1, D_MODEL), lambda i, j: (0, 0)),
                  pl.BlockSpec((D_MODEL, tf), lambda i, j: (0, j)),
                  pl.BlockSpec((D_MODEL, tf), lambda i, j: (0, j + nf)),
                  pl.BlockSpec((tf, D_MODEL), lambda i, j: (j, 0)),
                  pl.BlockSpec((1, D_MODEL), lambda i, j: (0, 0))],
        out_specs=pl.BlockSpec((tm, D_MODEL), lambda i, j: (i, 0)),
        out_shape=jax.ShapeDtypeStruct((t, D_MODEL), F32),
        scratch_shapes=[pltpu.VMEM((tm, D_MODEL), BF16), pltpu.VMEM((tm, D_MODEL), F32)],
        compiler_params=_cparams(("parallel", "arbitrary")),
        name="ffn",
    )(x, nw, wgu, wgu, wd, fw)


def _pick(n, pref):
    return pref if n % pref == 0 else n


def _rope_tables(positions):
    b, s = positions.shape
    inv_freq = 1.0 / (ROPE_THETA ** (jnp.arange(0, MLA_ROPE, 2, dtype=F32) / MLA_ROPE))
    ang = positions.astype(F32)[..., None] * inv_freq
    cos = jnp.cos(ang).reshape(b * s, MLA_ROPE // 2)
    sin = jnp.sin(ang).reshape(b * s, MLA_ROPE // 2)
    z64 = jnp.zeros((b * s, MLA_NOPE), F32)
    z32 = jnp.zeros((b * s, 32), F32)
    ck = jnp.concatenate([z64, cos, cos, z32], axis=1)
    sk = jnp.concatenate([z64, sin, sin, z32], axis=1)
    qscale = (MLA_QK ** -0.5) * LOG2E
    cq = jnp.concatenate([jnp.ones_like(z64), cos, cos, z32], axis=1) * qscale
    sq = sk * qscale
    to_t = lambda a: a.reshape(b, s, 128).transpose(0, 2, 1)
    return ck, sk, to_t(cq), to_t(sq)


def _layer_weights(w_in, w_uq, w_ukv):
    d = w_in.shape[0]
    z = lambda n: jnp.zeros((d, n), w_in.dtype)
    kr = w_in[:, 512:544]
    w_re = jnp.concatenate([
        w_in[:, 3136:6208], w_in[:, 544:2080], w_in[:, 2080:2592], w_in[:, 2624:3136],
        w_in[:, 0:256], w_in[:, 256:512],
        w_in[:, 2592:2624], z(32), kr, z(32),
        z(64), -kr[:, 16:32], kr[:, 0:16], z(32)], axis=1).astype(BF16)

    uq = w_uq.reshape(MLA_Q_LORA, MLA_HEADS, MLA_QK)
    zq = lambda n: jnp.zeros((MLA_Q_LORA, MLA_HEADS, n), w_uq.dtype)
    x1, x2 = uq[..., 64:80], uq[..., 80:96]
    wq1 = jnp.concatenate([uq[..., :64], x1, x2, zq(32)], axis=-1).reshape(MLA_Q_LORA, 1024)
    wq2 = jnp.concatenate([zq(64), -x2, x1, zq(32)], axis=-1).reshape(MLA_Q_LORA, 1024)
    ukv = w_ukv.reshape(MLA_KV_LORA, MLA_HEADS, MLA_NOPE + MLA_V)
    wk = jnp.concatenate([ukv[..., :64], jnp.zeros((MLA_KV_LORA, MLA_HEADS, 64), w_ukv.dtype)],
                         axis=-1).reshape(MLA_KV_LORA, 1024)
    wv = ukv[..., 64:].reshape(MLA_KV_LORA, MLA_HEADS * MLA_V)
    return w_re, wq1.T.astype(BF16), wq2.T.astype(BF16), wk.astype(BF16), wv.T.astype(BF16)


def kernel(x, mem, positions, norm_mix, w_in, mla_q_norm, mla_w_uq, mla_kv_norm, mla_w_ukv, dn_conv, dn_a_log, dn_dt_bias, dn_out_norm, mem_norm, mem_w_kv, w_branch_mla, w_branch_dn, w_branch_mem, w_out, norm_ffn, ffn_w_gate_up, ffn_w_down, final_norm):
    b, s, d = x.shape
    t = b * s
    depth = w_in.shape[0]
    mt = mem.shape[1]
    ck, sk, cqt, sqt = _rope_tables(positions)
    gi = jnp.arange(DN_W) // DN_DK
    gmat = (gi[:, None] == gi[None, :]).astype(BF16)
    xf = x.reshape(t, d)
    memf = mem.reshape(b * mt, d)
    tm = _pick(s, 512)
    for l in range(depth):
        w_re, wq1t, wq2t, wk, wvt = _layer_weights(w_in[l], mla_w_uq[l], mla_w_ukv[l])
        proj = _norm_matmul(xf, norm_mix[l][None], w_re, tm=_pick(t, 1024), tn=640, out_dtype=F32)

        qt, kk, vt = _mla_prep(proj, ck, sk, cqt, sqt, mla_q_norm[l][None], mla_kv_norm[l][None],
                               wq1t, wq2t, wk, wvt, b=b, s=s, tm=tm)
        o_mla = _flash(qt, kk, vt, b=b, s=s, tq=_pick(s, 512), tk=_pick(s, 256))

        conv_w = jnp.concatenate([dn_conv[l], jnp.zeros((8 - DN_CONV, 3 * DN_W), F32)], axis=0)
        pad16 = lambda v: jnp.concatenate([v.reshape(-1), jnp.zeros((128 - 2 * DN_HEADS,), F32)])
        par = jnp.concatenate([pad16(dn_a_log[l])[None], pad16(dn_dt_bias[l])[None],
                               jnp.zeros((6, 128), F32)], axis=0)
        qn, kn, vv, gb = _dn_prep(proj, conv_w, par, gmat, b=b, s=s, tm=tm)
        o_f, o_b = _dn_chunk(qn, kn, vv, gb, b=b, s=s, nch=4)
        n_chunks = s // DN_CHUNK
        perm = lambda a: a.reshape(b, n_chunks, DN_CHUNK, DN_W).swapaxes(1, 2).reshape(t, DN_W)
        o_f, o_b = perm(o_f), perm(o_b)

        memkv = _norm_matmul(memf, mem_norm[l][None], mem_w_kv[l].astype(BF16),
                             tm=b * mt, tn=512, out_dtype=BF16).reshape(b, mt, 2 * MEM_W)
        onorm = jnp.tile(dn_out_norm[l], DN_HEADS)[None]
        xf = _merge(xf, proj, o_mla, o_f, o_b, memkv, onorm, gmat,
                    w_branch_mla[l].astype(BF16), w_branch_dn[l].astype(BF16),
                    w_branch_mem[l].astype(BF16), w_out[l].astype(BF16), b=b, s=s, tm=tm)
        xf = _ffn(xf, norm_ffn[l][None], ffn_w_gate_up[l].astype(BF16), ffn_w_down[l].astype(BF16),
                  final_norm[None], tm=tm, tf=1408, final_norm=(l == depth - 1))
    return xf.reshape(b, s, d)
```

```python
import functools

import jax
import jax.numpy as jnp
from jax import lax
from jax.experimental import pallas as pl
from jax.experimental.pallas import tpu as pltpu

F32 = jnp.float32
BF16 = jnp.bfloat16

D_MODEL = 1024
EPS = 1e-6
MLA_HEADS = 8
MLA_Q_LORA = 256
MLA_KV_LORA = 256
MLA_NOPE = 64
MLA_ROPE = 32
MLA_V = 64
MLA_QK = MLA_NOPE + MLA_ROPE
ROPE_THETA = 10000.0
DN_HEADS = 8
DN_DK = 64
DN_W = DN_HEADS * DN_DK
DN_CONV = 5
DN_CHUNK = 64
MEM_HEADS = 4
MEM_HD = 128
MEM_W = MEM_HEADS * MEM_HD
D_FF = 2816

COL_GATES = 0
COL_DQKV = 3072
COL_DZ = 4608
COL_MQ = 5120
COL_CQ = 5632
COL_CKV = 5888
COL_SMALL = 6144
COL_SMALL2 = 6272
IN_COLS_P = 6400

HEAD_SLAB = 128
VT_ROWS = 80
LOG2E = 1.4426950408889634
VMEM_LIMIT = 48 * 1024 * 1024


def _cparams(sem):
    return pltpu.CompilerParams(dimension_semantics=sem, vmem_limit_bytes=VMEM_LIMIT)


def _sigmoid(x):
    return 1.0 / (1.0 + jnp.exp(-x))


def _dot(a, b):
    return jnp.dot(a, b, preferred_element_type=F32)


def _dot_nt(a, b):
    return lax.dot_general(a, b, (((1,), (1,)), ((), ())), preferred_element_type=F32)


def _dot_tn(a, b):
    return lax.dot_general(a, b, (((0,), (0,)), ((), ())), preferred_element_type=F32)


def _split3(x):
    x1 = x.astype(BF16)
    r1 = x - x1.astype(F32)
    x2 = r1.astype(BF16)
    x3 = (r1 - x2.astype(F32)).astype(BF16)
    return x1, x2, x3


def _rms(x, w):
    return x * lax.rsqrt(jnp.mean(x * x, axis=-1, keepdims=True) + EPS) * w


def _norm_matmul_kernel(x_ref, g_ref, w_ref, o_ref, h_ref):
    @pl.when(pl.program_id(1) == 0)
    def _():
        h_ref[...] = _rms(x_ref[...], g_ref[...]).astype(BF16)

    o_ref[...] = _dot(h_ref[...], w_ref[...]).astype(o_ref.dtype)


def _norm_matmul(x, g, w, *, tm, tn, out_dtype):
    t, d = x.shape
    n = w.shape[1]
    return pl.pallas_call(
        _norm_matmul_kernel,
        grid=(t // tm, n // tn),
        in_specs=[pl.BlockSpec((tm, d), lambda i, j: (i, 0)),
                  pl.BlockSpec((1, d), lambda i, j: (0, 0)),
                  pl.BlockSpec((d, tn), lambda i, j: (0, j))],
        out_specs=pl.BlockSpec((tm, tn), lambda i, j: (i, j)),
        out_shape=jax.ShapeDtypeStruct((t, n), out_dtype),
        scratch_shapes=[pltpu.VMEM((tm, d), BF16)],
        compiler_params=_cparams(("parallel", "arbitrary")),
        name="norm_matmul",
    )(x, g, w)


def _mla_prep_kernel(cq_ref, ckv_ref, sm_ref, ck_ref, sk_ref, cqt_ref, sqt_ref, qn_ref, kvn_ref,
                     wq1t_ref, wq2t_ref, wk_ref, wvt_ref, qt_out, k_out, vt_out):
    qn = _rms(cq_ref[...], qn_ref[...]).astype(BF16)
    q1t = _dot_nt(wq1t_ref[...], qn)
    q2t = _dot_nt(wq2t_ref[...], qn)
    ct = cqt_ref[0]
    st = sqt_ref[0]
    for h in range(MLA_HEADS):
        hs = slice(h * HEAD_SLAB, (h + 1) * HEAD_SLAB)
        qt_out[0, hs, :] = (q1t[hs] * ct + q2t[hs] * st).astype(BF16)

    kvn = _rms(ckv_ref[...], kvn_ref[...]).astype(BF16)
    kk = _dot(kvn, wk_ref[...])
    sm = sm_ref[...]
    kr = sm[:, :128] * ck_ref[...] + sm[:, 128:] * sk_ref[...]
    for h in range(MLA_HEADS):
        hs = slice(h * HEAD_SLAB, (h + 1) * HEAD_SLAB)
        k_out[:, hs] = (kk[:, hs] + kr).astype(BF16)

    vt = _dot_nt(wvt_ref[...], kvn)
    tm = vt.shape[1]
    ones = jnp.ones((VT_ROWS - MLA_V, tm), BF16)
    for h in range(MLA_HEADS):
        vt_out[0, h * VT_ROWS:h * VT_ROWS + MLA_V, :] = vt[h * MLA_V:(h + 1) * MLA_V].astype(BF16)
        vt_out[0, h * VT_ROWS + MLA_V:(h + 1) * VT_ROWS, :] = ones


def _mla_prep(proj, ck, sk, cqt, sqt, qnorm, kvnorm, wq1t, wq2t, wk, wvt, *, b, s, tm):
    nb = s // tm
    t = b * s
    row = lambda bi, i: bi * nb + i
    const = lambda shape: pl.BlockSpec(shape, lambda bi, i: (0,) * len(shape))
    return pl.pallas_call(
        _mla_prep_kernel,
        grid=(b, nb),
        in_specs=[pl.BlockSpec((tm, 256), lambda bi, i: (row(bi, i), COL_CQ // 256)),
                  pl.BlockSpec((tm, 256), lambda bi, i: (row(bi, i), COL_CKV // 256)),
                  pl.BlockSpec((tm, 256), lambda bi, i: (row(bi, i), COL_SMALL // 256)),
                  pl.BlockSpec((tm, 128), lambda bi, i: (row(bi, i), 0)),
                  pl.BlockSpec((tm, 128), lambda bi, i: (row(bi, i), 0)),
                  pl.BlockSpec((1, 128, tm), lambda bi, i: (bi, 0, i)),
                  pl.BlockSpec((1, 128, tm), lambda bi, i: (bi, 0, i)),
                  const((1, 256)), const((1, 256)),
                  const((1024, 256)), const((1024, 256)), const((256, 1024)), const((512, 256))],
        out_specs=[pl.BlockSpec((1, 1024, tm), lambda bi, i: (bi, 0, i)),
                   pl.BlockSpec((tm, 1024), lambda bi, i: (row(bi, i), 0)),
                   pl.BlockSpec((1, MLA_HEADS * VT_ROWS, tm), lambda bi, i: (bi, 0, i))],
        out_shape=[jax.ShapeDtypeStruct((b, 1024, s), BF16),
                   jax.ShapeDtypeStruct((t, 1024), BF16),
                   jax.ShapeDtypeStruct((b, MLA_HEADS * VT_ROWS, s), BF16)],
        compiler_params=_cparams(("parallel", "parallel")),
        name="mla_prep",
    )(proj, proj, proj, ck, sk, cqt, sqt, qnorm, kvnorm, wq1t, wq2t, wk, wvt)


def _chunk_start(j, tk):
    return j * tk if isinstance(j, int) else pl.multiple_of(j * tk, tk)


def _flash_kernel(qt_ref, k_ref, vt_ref, o_ref, m_ref, acc_ref, s_buf, p_buf, a_buf, *, tk, nk, unroll):
    m_ref[...] = jnp.full(m_ref.shape, -jnp.inf, F32)
    acc_ref[...] = jnp.zeros(acc_ref.shape, F32)

    def scores(j, par):
        r0 = _chunk_start(j, tk)
        for h in range(2):
            hs = slice(h * HEAD_SLAB, (h + 1) * HEAD_SLAB)
            s_buf[par, h] = _dot(k_ref[pl.ds(r0, tk), hs], qt_ref[0, hs, :]).astype(s_buf.dtype)

    def softmax(par):
        for h in range(2):
            st = s_buf[par, h]
            m_old = m_ref[h]
            m_new = jnp.maximum(m_old, jnp.max(st, axis=0, keepdims=True).astype(F32))
            a_buf[par, h] = jnp.exp2(m_old - m_new)
            p_buf[par, h] = jnp.exp2(st - m_new.astype(st.dtype)).astype(BF16)
            m_ref[h] = m_new

    def pv(j, par):
        r0 = _chunk_start(j, tk)
        for h in range(2):
            vc = vt_ref[0, h * VT_ROWS:(h + 1) * VT_ROWS, pl.ds(r0, tk)]
            acc_ref[h] = a_buf[par, h] * acc_ref[h] + _dot(vc, p_buf[par, h])

    def stage(j, par, have_next, have_next2):
        pv(j, par)
        if have_next:
            softmax(1 - par)
        if have_next2:
            scores(j + 2, par)

    scores(0, 0)
    scores(1, 1)
    softmax(0)
    trips = (nk - 2) // unroll

    def body(i, carry):
        for u in range(unroll):
            stage(unroll * i + u, u % 2, True, True)
        return carry

    lax.fori_loop(0, trips, body, 0)
    for j in range(trips * unroll, nk):
        stage(j, j % 2, j + 1 < nk, j + 2 < nk)

    outs = []
    for h in range(2):
        a = acc_ref[h]
        outs.append(a[0:MLA_V] / a[MLA_V:MLA_V + 1])
    ot = jnp.concatenate(outs, axis=0)
    o_ref[...] = ot.T.astype(o_ref.dtype)


def _flash(qt, k, vt, *, b, s, tq, tk, unroll=4, s_dtype=BF16):
    nq = s // tq
    nk = s // tk
    hp = MLA_HEADS // 2
    return pl.pallas_call(
        functools.partial(_flash_kernel, tk=tk, nk=nk, unroll=unroll),
        grid=(b, hp, nq),
        in_specs=[pl.BlockSpec((1, 2 * HEAD_SLAB, tq), lambda bi, p, qi: (bi, p, qi)),
                  pl.BlockSpec((s, 2 * HEAD_SLAB), lambda bi, p, qi: (bi, p)),
                  pl.BlockSpec((1, 2 * VT_ROWS, s), lambda bi, p, qi: (bi, p, 0))],
        out_specs=pl.BlockSpec((tq, 2 * MLA_V), lambda bi, p, qi: (bi * nq + qi, p)),
        out_shape=jax.ShapeDtypeStruct((b * s, MLA_HEADS * MLA_V), BF16),
        scratch_shapes=[pltpu.VMEM((2, 1, tq), F32), pltpu.VMEM((2, VT_ROWS, tq), F32),
                        pltpu.VMEM((2, 2, tk, tq), s_dtype), pltpu.VMEM((2, 2, tk, tq), BF16),
                        pltpu.VMEM((2, 2, 1, tq), F32)],
        compiler_params=_cparams(("parallel", "parallel", "arbitrary")),
        name="mla_flash",
    )(qt, k, vt)


def _group_sum(z, gmat):
    z1 = z.astype(BF16)
    z2 = (z - z1.astype(F32)).astype(BF16)
    return _dot(z1, gmat) + _dot(z2, gmat)


def _dn_prep_kernel(x_ref, xp_ref, xn_ref, sm_ref, cw_ref, par_ref, gm_ref,
                    q_out, k_out, v_out, gb_out, pad_ref, *, nb, tm):
    i = pl.program_id(0)
    first = (i % nb) == 0
    last = (i % nb) == nb - 1
    pad_ref[0:8, :] = jnp.where(first, 0.0, xp_ref[...])
    pad_ref[8:8 + tm, :] = x_ref[...]
    pad_ref[8 + tm:16 + tm, :] = jnp.where(last, 0.0, xn_ref[...])
    cw = cw_ref[...]
    y = cw[0:1] * pad_ref[pl.ds(6, tm), :]
    for j in range(1, DN_CONV):
        y = y + cw[j:j + 1] * pad_ref[pl.ds(6 + j, tm), :]
    y = y * _sigmoid(y)
    gm = gm_ref[...]
    q = y[:, 0:DN_W]
    k = y[:, DN_W:2 * DN_W]
    q_out[...] = q * lax.rsqrt(_group_sum(q * q, gm) + EPS) * (DN_DK ** -0.5)
    k_out[...] = k * lax.rsqrt(_group_sum(k * k, gm) + EPS)
    v_out[...] = y[:, 2 * DN_W:]

    sm = sm_ref[...]
    par = par_ref[...]
    z = sm + par[1:2]
    softplus = jnp.maximum(z, 0.0) + jnp.log1p(jnp.exp(-jnp.abs(z)))
    g = -jnp.exp(par[0:1]) * softplus
    lane = lax.broadcasted_iota(jnp.int32, sm.shape, 1)
    gb_out[...] = jnp.where(lane < 16, g, _sigmoid(sm))


def _dn_prep(proj, conv_w, par, gmat, *, b, s, tm):
    t = b * s
    nb = s // tm
    w3 = 3 * DN_W
    cb = COL_DQKV // w3
    hb = tm // 8
    last_hb = t // 8 - 1
    return pl.pallas_call(
        functools.partial(_dn_prep_kernel, nb=nb, tm=tm),
        grid=(t // tm,),
        in_specs=[pl.BlockSpec((tm, w3), lambda i: (i, cb)),
                  pl.BlockSpec((8, w3), lambda i: (jnp.maximum(i * hb - 1, 0), cb)),
                  pl.BlockSpec((8, w3), lambda i: (jnp.minimum((i + 1) * hb, last_hb), cb)),
                  pl.BlockSpec((tm, 128), lambda i: (i, COL_SMALL // 128)),
                  pl.BlockSpec((8, w3), lambda i: (0, 0)),
                  pl.BlockSpec((8, 128), lambda i: (0, 0)),
                  pl.BlockSpec((DN_W, DN_W), lambda i: (0, 0))],
        out_specs=[pl.BlockSpec((tm, DN_W), lambda i: (i, 0)),
                   pl.BlockSpec((tm, DN_W), lambda i: (i, 0)),
                   pl.BlockSpec((tm, DN_W), lambda i: (i, 0)),
                   pl.BlockSpec((tm, 128), lambda i: (i, 0))],
        out_shape=[jax.ShapeDtypeStruct((t, DN_W), F32)] * 3 + [jax.ShapeDtypeStruct((t, 128), F32)],
        scratch_shapes=[pltpu.VMEM((tm + 16, w3), F32)],
        compiler_params=_cparams(("parallel",)),
        name="dn_prep",
    )(proj, proj, proj, proj, conv_w, par, gmat)


GRP = 256
NGRP = DN_W // GRP


def _bd_rows(x_bf, bmask):
    return jnp.where(bmask, jnp.concatenate([x_bf] * (GRP // DN_CHUNK), axis=0), jnp.zeros((), BF16))


def _dn_chunk_kernel(qf_ref, kf_ref, vf_ref, gf_ref, qb_ref, kb_ref, vb_ref, gb_ref,
                     of_ref, ob_ref, s_ref, *, nch):
    c = DN_CHUNK

    @pl.when(pl.program_id(1) == 0)
    def _():
        s_ref[...] = jnp.zeros(s_ref.shape, F32)

    ri = lax.broadcasted_iota(jnp.int32, (c, GRP), 0)
    cj = lax.broadcasted_iota(jnp.int32, (c, GRP), 1) & (c - 1)
    br = lax.broadcasted_iota(jnp.int32, (GRP, GRP), 0) >> 6
    bc = lax.broadcasted_iota(jnp.int32, (GRP, GRP), 1) >> 6
    bmask = br == bc
    ident = (ri == cj).astype(F32)
    incl = (ri >= cj, ri <= cj)
    strict = (ri > cj, ri < cj)

    def pmm(a, bmat):
        return _dot(a.astype(BF16), _bd_rows(bmat.astype(BF16), bmask))

    def pmm_nt(a, bmat):
        return _dot_nt(a.astype(BF16), _bd_rows(bmat.astype(BF16), bmask))

    er = lax.broadcasted_iota(jnp.int32, (128, DN_W), 0)
    ec = lax.broadcasted_iota(jnp.int32, (128, DN_W), 1) >> 6
    tr = lax.broadcasted_iota(jnp.int32, (c, c), 0)
    tc = lax.broadcasted_iota(jnp.int32, (c, c), 1)
    mr = lax.broadcasted_iota(jnp.int32, (c, DN_W), 0)
    mj = lax.broadcasted_iota(jnp.int32, (c, DN_W), 1) & (c - 1)
    ones_cc = jnp.ones((c, c), BF16)
    tri = ((tc <= tr).astype(BF16), (tc >= tr).astype(BF16))
    mask_t = (mr <= mj, mr >= mj)
    refs = ((qf_ref, kf_ref, vf_ref, gf_ref, of_ref), (qb_ref, kb_ref, vb_ref, gb_ref, ob_ref))

    def rows(d, step):
        i = step if d == 0 else nch - 1 - step
        return slice(i * c, (i + 1) * c)

    def sum3(lhs, rhs3):
        return _dot(lhs, rhs3[0]) + _dot(lhs, rhs3[1]) + _dot(lhs, rhs3[2])

    ds = [(d, step) for step in range(nch) for d in (0, 1)]
    g3 = {u: _split3(refs[u[0]][3][rows(*u), :]) for u in ds}
    e_g = [(er == d * DN_HEADS + ec).astype(BF16) for d in (0, 1)]
    e_b = [(er == 16 + d * DN_HEADS + ec).astype(BF16) for d in (0, 1)]
    gexp = {u: _dot(g3[u][0], e_g[u[0]]) + _dot(g3[u][1], e_g[u[0]]) + _dot(g3[u][2], e_g[u[0]]) for u in ds}
    bexp = {u: _dot(g3[u][0], e_b[u[0]]) + _dot(g3[u][1], e_b[u[0]]) + _dot(g3[u][2], e_b[u[0]]) for u in ds}
    ge3 = {u: _split3(gexp[u]) for u in ds}
    gm3 = {u: _split3(jnp.where(mask_t[u[0]], gexp[u], 0.0)) for u in ds}
    gcrow = {u: sum3(tri[u[0]], ge3[u]) for u in ds}
    gccol = {u: sum3(ones_cc, gm3[u]) for u in ds}

    units = [(d, step, g) for step in range(nch) for d in (0, 1) for g in range(NGRP)]
    sl = lambda g: slice(g * GRP, (g + 1) * GRP)
    kg = {u: refs[u[0]][1][rows(u[0], u[1]), sl(u[2])] for u in units}
    qg = {u: refs[u[0]][0][rows(u[0], u[1]), sl(u[2])] for u in units}
    vg = {u: refs[u[0]][2][rows(u[0], u[1]), sl(u[2])] for u in units}
    bg = {u: bexp[u[:2]][:, sl(u[2])] for u in units}
    gc = {u: gcrow[u[:2]][:, sl(u[2])] for u in units}
    dec = {u: jnp.exp(jnp.where(incl[u[0]], gc[u] - gccol[u[:2]][:, sl(u[2])], -jnp.inf)) for u in units}
    egc = {u: jnp.exp(gc[u]) for u in units}
    kk = {u: pmm_nt(kg[u], kg[u]) for u in units}
    qk = {u: pmm_nt(qg[u], kg[u]) for u in units}
    lm = {u: jnp.where(strict[u[0]], kk[u] * bg[u] * dec[u], 0.0) for u in units}

    dd = {u: jnp.where((ri >> 3) == (cj >> 3), lm[u], 0.0) for u in units}
    d2 = {u: pmm(dd[u], dd[u]) for u in units}
    d4 = {u: pmm(d2[u], d2[u]) for u in units}
    x = {u: pmm(ident - dd[u], ident + d2[u]) for u in units}
    x = {u: pmm(x[u], ident + d4[u]) for u in units}
    for sh in (3, 4, 5):
        off = ((ri >> (sh + 1)) == (cj >> (sh + 1))) & ((ri >> sh) != (cj >> sh))
        xm = {u: pmm(x[u], jnp.where(off, lm[u], 0.0)) for u in units}
        x = {u: x[u] - pmm(xm[u], x[u]) for u in units}
    uu = {u: pmm(x[u], vg[u] * bg[u]) for u in units}
    ww = {u: pmm(x[u], kg[u] * bg[u] * egc[u]).astype(BF16) for u in units}
    att = {u: jnp.where(incl[u[0]], qk[u] * dec[u], 0.0).astype(BF16) for u in units}
    qe = {u: (qg[u] * egc[u]).astype(BF16) for u in units}
    gl = {u: (gc[u][c - 1:c] if u[0] == 0 else gc[u][0:1]) for u in units}
    kdec = {u: (kg[u] * jnp.exp(gl[u] - gc[u])).astype(BF16) for u in units}
    egl = {u: jnp.exp(gl[u]) for u in units}

    for step in range(nch):
        cur = [(d, step, g) for d in (0, 1) for g in range(NGRP)]
        st = {u: s_ref[u[0] * NGRP + u[2]] for u in cur}
        sb = {u: st[u].astype(BF16) for u in cur}
        vnew = {u: uu[u] - _dot(ww[u], sb[u]) for u in cur}
        vbd = {u: _bd_rows(vnew[u].astype(BF16), bmask) for u in cur}
        for u in cur:
            refs[u[0]][4][rows(u[0], step), sl(u[2])] = _dot(qe[u], sb[u]) + _dot(att[u], vbd[u])
        upd = {u: _dot_tn(kdec[u], vnew[u].astype(BF16)) for u in cur}
        for u in cur:
            s_ref[u[0] * NGRP + u[2]] = st[u] * egl[u] + jnp.where(bmask, upd[u], 0.0)


def _dn_chunk(qn, kn, vv, gb, *, b, s, nch):
    n = s // (DN_CHUNK * nch)
    t = b * s
    fwd = lambda bi, ci: (bi * n + ci, 0)
    bwd = lambda bi, ci: (bi * n + n - 1 - ci, 0)
    wide = lambda im: pl.BlockSpec((DN_CHUNK * nch, DN_W), im)
    small = lambda im: pl.BlockSpec((DN_CHUNK * nch, 128), im)
    return pl.pallas_call(
        functools.partial(_dn_chunk_kernel, nch=nch),
        grid=(b, n),
        in_specs=[wide(fwd), wide(fwd), wide(fwd), small(fwd), wide(bwd), wide(bwd), wide(bwd), small(bwd)],
        out_specs=[wide(fwd), wide(bwd)],
        out_shape=[jax.ShapeDtypeStruct((t, DN_W), F32)] * 2,
        scratch_shapes=[pltpu.VMEM((2 * NGRP, GRP, GRP), F32)],
        compiler_params=_cparams(("parallel", "arbitrary")),
        name="dn_chunk",
    )(qn, kn, vv, gb, qn, kn, vv, gb)


def _merge_kernel(x_ref, gates_ref, dz_ref, mq_ref, omla_ref, of_ref, ob_ref, mk_ref, mv_ref,
                  onorm_ref, gm_ref, wm_ref, wd_ref, wmem_ref, wout_ref, o_ref):
    o = of_ref[...] + ob_ref[...]
    ms = _group_sum(o * o, gm_ref[...]) * (1.0 / DN_DK)
    dz = dz_ref[...]
    o_dn = o * lax.rsqrt(ms + EPS) * onorm_ref[...] * (dz * _sigmoid(dz))

    mq = mq_ref[...]
    mk = mk_ref[0]
    mv = mv_ref[0]
    outs = []
    for h in range(MEM_HEADS):
        hs = slice(h * MEM_HD, (h + 1) * MEM_HD)
        sc = _dot_nt(mq[:, hs].astype(BF16), mk[:, hs]) * (MEM_HD ** -0.5)
        p = jnp.exp(sc - jnp.max(sc, axis=-1, keepdims=True))
        p = p / jnp.sum(p, axis=-1, keepdims=True)
        outs.append(_dot(p.astype(BF16), mv[:, hs]))
    o_mem = jnp.concatenate(outs, axis=-1)

    gates = gates_ref[...]
    merged = (_sigmoid(gates[:, 0:D_MODEL]) * _dot(omla_ref[...], wm_ref[...])
              + _sigmoid(gates[:, D_MODEL:2 * D_MODEL]) * _dot(o_dn.astype(BF16), wd_ref[...])
              + _sigmoid(gates[:, 2 * D_MODEL:]) * _dot(o_mem.astype(BF16), wmem_ref[...]))
    o_ref[...] = x_ref[...] + _dot(merged.astype(BF16), wout_ref[...])


def _merge(x, proj, omla, o_f, o_b, memkv, onorm, gmat, wm, wd, wmem, wout, *, b, s, tm):
    t = b * s
    nb = s // tm
    mt = memkv.shape[1]
    const = lambda shape: pl.BlockSpec(shape, lambda i: (0,) * len(shape))
    return pl.pallas_call(
        _merge_kernel,
        grid=(t // tm,),
        in_specs=[pl.BlockSpec((tm, D_MODEL), lambda i: (i, 0)),
                  pl.BlockSpec((tm, 3 * D_MODEL), lambda i: (i, COL_GATES // (3 * D_MODEL))),
                  pl.BlockSpec((tm, DN_W), lambda i: (i, COL_DZ // DN_W)),
                  pl.BlockSpec((tm, MEM_W), lambda i: (i, COL_MQ // MEM_W)),
                  pl.BlockSpec((tm, DN_W), lambda i: (i, 0)),
                  pl.BlockSpec((tm, DN_W), lambda i: (i, 0)),
                  pl.BlockSpec((tm, DN_W), lambda i: (i, 0)),
                  pl.BlockSpec((1, mt, MEM_W), lambda i: (i // nb, 0, 0)),
                  pl.BlockSpec((1, mt, MEM_W), lambda i: (i // nb, 0, 1)),
                  const((1, DN_W)), const((DN_W, DN_W)),
                  const((DN_W, D_MODEL)), const((DN_W, D_MODEL)), const((MEM_W, D_MODEL)),
                  const((D_MODEL, D_MODEL))],
        out_specs=pl.BlockSpec((tm, D_MODEL), lambda i: (i, 0)),
        out_shape=jax.ShapeDtypeStruct((t, D_MODEL), F32),
        compiler_params=_cparams(("parallel",)),
        name="merge",
    )(x, proj, proj, proj, omla, o_f, o_b, memkv, memkv, onorm, gmat, wm, wd, wmem, wout)


def _ffn_kernel(x_ref, nw_ref, wgu_ref, wd_ref, fw_ref, o_ref, *, nf, final_norm):
    x = x_ref[...]
    h = _rms(x, nw_ref[...]).astype(BF16)
    tf = D_FF // nf
    y = x
    for c in range(nf):
        gate = _dot(h, wgu_ref[:, c * tf:(c + 1) * tf])
        up = _dot(h, wgu_ref[:, D_FF + c * tf:D_FF + (c + 1) * tf])
        act = (gate * _sigmoid(gate) * up).astype(BF16)
        y = y + _dot(act, wd_ref[c * tf:(c + 1) * tf, :])
    if final_norm:
        y = _rms(y, fw_ref[...])
    o_ref[...] = y


def _ffn(x, nw, wgu, wd, fw, *, tm, nf, final_norm):
    t = x.shape[0]
    resident = lambda shape: pl.BlockSpec(shape, lambda i: (0, 0), pipeline_mode=pl.Buffered(1))
    return pl.pallas_call(
        functools.partial(_ffn_kernel, nf=nf, final_norm=final_norm),
        grid=(t // tm,),
        in_specs=[pl.BlockSpec((tm, D_MODEL), lambda i: (i, 0)),
                  pl.BlockSpec((1, D_MODEL), lambda i: (0, 0)),
                  resident((D_MODEL, 2 * D_FF)),
                  resident((D_FF, D_MODEL)),
                  pl.BlockSpec((1, D_MODEL), lambda i: (0, 0))],
        out_specs=pl.BlockSpec((tm, D_MODEL), lambda i: (i, 0)),
        out_shape=jax.ShapeDtypeStruct((t, D_MODEL), F32),
        compiler_params=_cparams(("parallel",)),
        name="ffn",
    )(x, nw, wgu, wd, fw)


def _pick(n, pref):
    return pref if n % pref == 0 else n


def _rope_tables(positions):
    b, s = positions.shape
    inv_freq = 1.0 / (ROPE_THETA ** (jnp.arange(0, MLA_ROPE, 2, dtype=F32) / MLA_ROPE))
    ang = positions.astype(F32)[..., None] * inv_freq
    cos = jnp.cos(ang).reshape(b * s, MLA_ROPE // 2)
    sin = jnp.sin(ang).reshape(b * s, MLA_ROPE // 2)
    z64 = jnp.zeros((b * s, MLA_NOPE), F32)
    z32 = jnp.zeros((b * s, 32), F32)
    ck = jnp.concatenate([z64, cos, cos, z32], axis=1)
    sk = jnp.concatenate([z64, sin, sin, z32], axis=1)
    qscale = (MLA_QK ** -0.5) * LOG2E
    cq = jnp.concatenate([jnp.ones_like(z64), cos, cos, z32], axis=1) * qscale
    sq = sk * qscale
    to_t = lambda a: a.reshape(b, s, 128).transpose(0, 2, 1)
    return ck, sk, to_t(cq), to_t(sq)


def _layer_weights(w_in, w_uq, w_ukv):
    d = w_in.shape[0]
    z = lambda n: jnp.zeros((d, n), w_in.dtype)
    kr = w_in[:, 512:544]
    w_re = jnp.concatenate([
        w_in[:, 3136:6208], w_in[:, 544:2080], w_in[:, 2080:2592], w_in[:, 2624:3136],
        w_in[:, 0:256], w_in[:, 256:512],
        w_in[:, 2592:2624], z(32), kr, z(32),
        z(64), -kr[:, 16:32], kr[:, 0:16], z(32)], axis=1).astype(BF16)

    uq = w_uq.reshape(MLA_Q_LORA, MLA_HEADS, MLA_QK)
    zq = lambda n: jnp.zeros((MLA_Q_LORA, MLA_HEADS, n), w_uq.dtype)
    x1, x2 = uq[..., 64:80], uq[..., 80:96]
    wq1 = jnp.concatenate([uq[..., :64], x1, x2, zq(32)], axis=-1).reshape(MLA_Q_LORA, 1024)
    wq2 = jnp.concatenate([zq(64), -x2, x1, zq(32)], axis=-1).reshape(MLA_Q_LORA, 1024)
    ukv = w_ukv.reshape(MLA_KV_LORA, MLA_HEADS, MLA_NOPE + MLA_V)
    wk = jnp.concatenate([ukv[..., :64], jnp.zeros((MLA_KV_LORA, MLA_HEADS, 64), w_ukv.dtype)],
                         axis=-1).reshape(MLA_KV_LORA, 1024)
    wv = ukv[..., 64:].reshape(MLA_KV_LORA, MLA_HEADS * MLA_V)
    return w_re, wq1.T.astype(BF16), wq2.T.astype(BF16), wk.astype(BF16), wv.T.astype(BF16)


def kernel(x, mem, positions, norm_mix, w_in, mla_q_norm, mla_w_uq, mla_kv_norm, mla_w_ukv, dn_conv, dn_a_log, dn_dt_bias, dn_out_norm, mem_norm, mem_w_kv, w_branch_mla, w_branch_dn, w_branch_mem, w_out, norm_ffn, ffn_w_gate_up, ffn_w_down, final_norm):
    b, s, d = x.shape
    t = b * s
    depth = w_in.shape[0]
    mt = mem.shape[1]
    ck, sk, cqt, sqt = _rope_tables(positions)
    gi = jnp.arange(DN_W) // DN_DK
    gmat = (gi[:, None] == gi[None, :]).astype(BF16)
    xf = x.reshape(t, d)
    memf = mem.reshape(b * mt, d)
    tm = _pick(s, 512)
    for l in range(depth):
        w_re, wq1t, wq2t, wk, wvt = _layer_weights(w_in[l], mla_w_uq[l], mla_w_ukv[l])
        proj = _norm_matmul(xf, norm_mix[l][None], w_re, tm=_pick(t, 1024), tn=1280, out_dtype=F32)

        qt, kk, vt = _mla_prep(proj, ck, sk, cqt, sqt, mla_q_norm[l][None], mla_kv_norm[l][None],
                               wq1t, wq2t, wk, wvt, b=b, s=s, tm=tm)
        o_mla = _flash(qt, kk, vt, b=b, s=s, tq=_pick(s, 512), tk=_pick(s, 256))

        conv_w = jnp.concatenate([dn_conv[l], jnp.zeros((8 - DN_CONV, 3 * DN_W), F32)], axis=0)
        pad16 = lambda v: jnp.concatenate([v.reshape(-1), jnp.zeros((128 - 2 * DN_HEADS,), F32)])
        par = jnp.concatenate([pad16(dn_a_log[l])[None], pad16(dn_dt_bias[l])[None],
                               jnp.zeros((6, 128), F32)], axis=0)
        qn, kn, vv, gb = _dn_prep(proj, conv_w, par, gmat, b=b, s=s, tm=tm)
        o_f, o_b = _dn_chunk(qn, kn, vv, gb, b=b, s=s, nch=4)
        n_chunks = s // DN_CHUNK
        perm = lambda a: a.reshape(b, n_chunks, DN_CHUNK, DN_W).swapaxes(1, 2).reshape(t, DN_W)
        o_f, o_b = perm(o_f), perm(o_b)

        memkv = _norm_matmul(memf, mem_norm[l][None], mem_w_kv[l].astype(BF16),
                             tm=b * mt, tn=512, out_dtype=BF16).reshape(b, mt, 2 * MEM_W)
        onorm = jnp.tile(dn_out_norm[l], DN_HEADS)[None]
        xf = _merge(xf, proj, o_mla, o_f, o_b, memkv, onorm, gmat,
                    w_branch_mla[l].astype(BF16), w_branch_dn[l].astype(BF16),
                    w_branch_mem[l].astype(BF16), w_out[l].astype(BF16), b=b, s=s, tm=tm)
        xf = _ffn(xf, norm_ffn[l][None], ffn_w_gate_up[l].astype(BF16), ffn_w_down[l].astype(BF16),
                  final_norm[None], tm=tm, nf=2, final_norm=(l == depth - 1))
    return xf.reshape(b, s, d)
```

```python
import functools

import jax
import jax.numpy as jnp
from jax import lax
from jax.experimental import pallas as pl
from jax.experimental.pallas import tpu as pltpu

F32 = jnp.float32
BF16 = jnp.bfloat16

D_MODEL = 1024
EPS = 1e-6
MLA_HEADS = 8
MLA_Q_LORA = 256
MLA_KV_LORA = 256
MLA_NOPE = 64
MLA_ROPE = 32
MLA_V = 64
MLA_QK = MLA_NOPE + MLA_ROPE
ROPE_THETA = 10000.0
DN_HEADS = 8
DN_DK = 64
DN_W = DN_HEADS * DN_DK
DN_CONV = 5
DN_CHUNK = 64
MEM_HEADS = 4
MEM_HD = 128
MEM_W = MEM_HEADS * MEM_HD
D_FF = 2816

COL_GATES = 0
COL_DQKV = 3072
COL_DZ = 4608
COL_MQ = 5120
COL_CQ = 5632
COL_CKV = 5888
COL_SMALL = 6144
COL_SMALL2 = 6272
IN_COLS_P = 6400

HEAD_SLAB = 128
VT_ROWS = 80
LOG2E = 1.4426950408889634
VMEM_LIMIT = 48 * 1024 * 1024


def _cparams(sem):
    return pltpu.CompilerParams(dimension_semantics=sem, vmem_limit_bytes=VMEM_LIMIT)


def _sigmoid(x):
    return 1.0 / (1.0 + jnp.exp(-x))


def _dot(a, b):
    return jnp.dot(a, b, preferred_element_type=F32)


def _dot_nt(a, b):
    return lax.dot_general(a, b, (((1,), (1,)), ((), ())), preferred_element_type=F32)


def _dot_tn(a, b):
    return lax.dot_general(a, b, (((0,), (0,)), ((), ())), preferred_element_type=F32)


def _split3(x):
    x1 = x.astype(BF16)
    r1 = x - x1.astype(F32)
    x2 = r1.astype(BF16)
    x3 = (r1 - x2.astype(F32)).astype(BF16)
    return x1, x2, x3


def _rms(x, w):
    return x * lax.rsqrt(jnp.mean(x * x, axis=-1, keepdims=True) + EPS) * w


def _norm_matmul_kernel(x_ref, g_ref, w_ref, o_ref, h_ref):
    @pl.when(pl.program_id(1) == 0)
    def _():
        h_ref[...] = _rms(x_ref[...], g_ref[...]).astype(BF16)

    o_ref[...] = _dot(h_ref[...], w_ref[...]).astype(o_ref.dtype)


def _norm_matmul(x, g, w, *, tm, tn, out_dtype):
    t, d = x.shape
    n = w.shape[1]
    return pl.pallas_call(
        _norm_matmul_kernel,
        grid=(t // tm, n // tn),
        in_specs=[pl.BlockSpec((tm, d), lambda i, j: (i, 0)),
                  pl.BlockSpec((1, d), lambda i, j: (0, 0)),
                  pl.BlockSpec((d, tn), lambda i, j: (0, j))],
        out_specs=pl.BlockSpec((tm, tn), lambda i, j: (i, j)),
        out_shape=jax.ShapeDtypeStruct((t, n), out_dtype),
        scratch_shapes=[pltpu.VMEM((tm, d), BF16)],
        compiler_params=_cparams(("parallel", "arbitrary")),
        name="norm_matmul",
    )(x, g, w)


def _mla_prep_kernel(cq_ref, ckv_ref, sm_ref, ck_ref, sk_ref, cqt_ref, sqt_ref, qn_ref, kvn_ref,
                     wq1t_ref, wq2t_ref, wk_ref, wvt_ref, qt_out, k_out, vt_out):
    qn = _rms(cq_ref[...], qn_ref[...]).astype(BF16)
    q1t = _dot_nt(wq1t_ref[...], qn)
    q2t = _dot_nt(wq2t_ref[...], qn)
    ct = cqt_ref[0]
    st = sqt_ref[0]
    for h in range(MLA_HEADS):
        hs = slice(h * HEAD_SLAB, (h + 1) * HEAD_SLAB)
        qt_out[0, hs, :] = (q1t[hs] * ct + q2t[hs] * st).astype(BF16)

    kvn = _rms(ckv_ref[...], kvn_ref[...]).astype(BF16)
    kk = _dot(kvn, wk_ref[...])
    sm = sm_ref[...]
    kr = sm[:, :128] * ck_ref[...] + sm[:, 128:] * sk_ref[...]
    for h in range(MLA_HEADS):
        hs = slice(h * HEAD_SLAB, (h + 1) * HEAD_SLAB)
        k_out[:, hs] = (kk[:, hs] + kr).astype(BF16)

    vt = _dot_nt(wvt_ref[...], kvn)
    tm = vt.shape[1]
    ones = jnp.ones((VT_ROWS - MLA_V, tm), BF16)
    for h in range(MLA_HEADS):
        vt_out[0, h * VT_ROWS:h * VT_ROWS + MLA_V, :] = vt[h * MLA_V:(h + 1) * MLA_V].astype(BF16)
        vt_out[0, h * VT_ROWS + MLA_V:(h + 1) * VT_ROWS, :] = ones


def _mla_prep(proj, ck, sk, cqt, sqt, qnorm, kvnorm, wq1t, wq2t, wk, wvt, *, b, s, tm):
    nb = s // tm
    t = b * s
    row = lambda bi, i: bi * nb + i
    const = lambda shape: pl.BlockSpec(shape, lambda bi, i: (0,) * len(shape))
    return pl.pallas_call(
        _mla_prep_kernel,
        grid=(b, nb),
        in_specs=[pl.BlockSpec((tm, 256), lambda bi, i: (row(bi, i), COL_CQ // 256)),
                  pl.BlockSpec((tm, 256), lambda bi, i: (row(bi, i), COL_CKV // 256)),
                  pl.BlockSpec((tm, 256), lambda bi, i: (row(bi, i), COL_SMALL // 256)),
                  pl.BlockSpec((tm, 128), lambda bi, i: (row(bi, i), 0)),
                  pl.BlockSpec((tm, 128), lambda bi, i: (row(bi, i), 0)),
                  pl.BlockSpec((1, 128, tm), lambda bi, i: (bi, 0, i)),
                  pl.BlockSpec((1, 128, tm), lambda bi, i: (bi, 0, i)),
                  const((1, 256)), const((1, 256)),
                  const((1024, 256)), const((1024, 256)), const((256, 1024)), const((512, 256))],
        out_specs=[pl.BlockSpec((1, 1024, tm), lambda bi, i: (bi, 0, i)),
                   pl.BlockSpec((tm, 1024), lambda bi, i: (row(bi, i), 0)),
                   pl.BlockSpec((1, MLA_HEADS * VT_ROWS, tm), lambda bi, i: (bi, 0, i))],
        out_shape=[jax.ShapeDtypeStruct((b, 1024, s), BF16),
                   jax.ShapeDtypeStruct((t, 1024), BF16),
                   jax.ShapeDtypeStruct((b, MLA_HEADS * VT_ROWS, s), BF16)],
        compiler_params=_cparams(("parallel", "parallel")),
        name="mla_prep",
    )(proj, proj, proj, ck, sk, cqt, sqt, qnorm, kvnorm, wq1t, wq2t, wk, wvt)


def _chunk_start(j, tk):
    return j * tk if isinstance(j, int) else pl.multiple_of(j * tk, tk)


def _flash_kernel(qt_ref, k_ref, vt_ref, o_ref, m_ref, acc_ref, s_buf, p_buf, a_buf, *, tq, tk, nq, nk, unroll):
    total = nq * nk
    lognk = nk.bit_length() - 1
    m_ref[...] = jnp.full(m_ref.shape, -jnp.inf, F32)
    acc_ref[...] = jnp.zeros(acc_ref.shape, F32)

    def split(n):
        if isinstance(n, int):
            return n // nk, n % nk
        return n >> lognk, n & (nk - 1)

    def scores(n, par):
        qi, j = split(n)
        q0 = _chunk_start(qi, tq)
        r0 = _chunk_start(j, tk)
        for h in range(2):
            hs = slice(h * HEAD_SLAB, (h + 1) * HEAD_SLAB)
            s_buf[par, h] = _dot(k_ref[pl.ds(r0, tk), hs],
                                 qt_ref[0, hs, pl.ds(q0, tq)]).astype(s_buf.dtype)

    def softmax(n, par):
        _, j = split(n)
        for h in range(2):
            st = s_buf[par, h]
            m_old = jnp.where(j == 0, -jnp.inf, m_ref[h])
            m_new = jnp.maximum(m_old, jnp.max(st, axis=0, keepdims=True).astype(F32))
            a_buf[par, h] = jnp.exp2(m_old - m_new)
            p_buf[par, h] = jnp.exp2(st - m_new.astype(st.dtype)).astype(BF16)
            m_ref[h] = m_new

    def pv(n, par):
        _, j = split(n)
        r0 = _chunk_start(j, tk)
        for h in range(2):
            vc = vt_ref[0, h * VT_ROWS:(h + 1) * VT_ROWS, pl.ds(r0, tk)]
            acc_ref[h] = a_buf[par, h] * acc_ref[h] + _dot(vc, p_buf[par, h])

    def finalize(qi):
        outs = []
        for h in range(2):
            a = acc_ref[h]
            outs.append(a[0:MLA_V] / a[MLA_V:MLA_V + 1])
        ot = jnp.concatenate(outs, axis=0)
        o_ref[pl.ds(_chunk_start(qi, tq), tq), :] = ot.T.astype(o_ref.dtype)

    scores(0, 0)
    scores(1, 1)
    softmax(0, 0)
    trips = (total - 2) // unroll
    assert trips == 0 or (nk % unroll == 0 and nk == 1 << lognk and unroll % 2 == 0)

    def body(i, carry):
        for u in range(unroll):
            n = unroll * i + u
            pv(n, u % 2)
            softmax(n + 1, (u + 1) % 2)
            scores(n + 2, u % 2)
        n_last = unroll * i + unroll - 1

        @pl.when((n_last & (nk - 1)) == nk - 1)
        def _():
            finalize(n_last >> lognk)
        return carry

    lax.fori_loop(0, trips, body, 0)
    for n in range(trips * unroll, total):
        pv(n, n % 2)
        if n + 1 < total:
            softmax(n + 1, (n + 1) % 2)
        if n + 2 < total:
            scores(n + 2, n % 2)
        if n % nk == nk - 1:
            finalize(n // nk)


def _flash(qt, k, vt, *, b, s, tq, tk, unroll=4, s_dtype=BF16):
    nq = s // tq
    nk = s // tk
    hp = MLA_HEADS // 2
    return pl.pallas_call(
        functools.partial(_flash_kernel, tq=tq, tk=tk, nq=nq, nk=nk, unroll=unroll),
        grid=(b, hp),
        in_specs=[pl.BlockSpec((1, 2 * HEAD_SLAB, s), lambda bi, p: (bi, p, 0)),
                  pl.BlockSpec((s, 2 * HEAD_SLAB), lambda bi, p: (bi, p)),
                  pl.BlockSpec((1, 2 * VT_ROWS, s), lambda bi, p: (bi, p, 0))],
        out_specs=pl.BlockSpec((s, 2 * MLA_V), lambda bi, p: (bi, p)),
        out_shape=jax.ShapeDtypeStruct((b * s, MLA_HEADS * MLA_V), BF16),
        scratch_shapes=[pltpu.VMEM((2, 1, tq), F32), pltpu.VMEM((2, VT_ROWS, tq), F32),
                        pltpu.VMEM((2, 2, tk, tq), s_dtype), pltpu.VMEM((2, 2, tk, tq), BF16),
                        pltpu.VMEM((2, 2, 1, tq), F32)],
        compiler_params=_cparams(("parallel", "arbitrary")),
        name="mla_flash",
    )(qt, k, vt)


def _group_sum(z, gmat):
    z1 = z.astype(BF16)
    z2 = (z - z1.astype(F32)).astype(BF16)
    return _dot(z1, gmat) + _dot(z2, gmat)


def _dn_prep_kernel(x_ref, xp_ref, xn_ref, sm_ref, cw_ref, par_ref, gm_ref,
                    q_out, k_out, v_out, gb_out, pad_ref, *, nb, tm):
    i = pl.program_id(0)
    first = (i % nb) == 0
    last = (i % nb) == nb - 1
    pad_ref[0:8, :] = jnp.where(first, 0.0, xp_ref[...])
    pad_ref[8:8 + tm, :] = x_ref[...]
    pad_ref[8 + tm:16 + tm, :] = jnp.where(last, 0.0, xn_ref[...])
    cw = cw_ref[...]
    y = cw[0:1] * pad_ref[pl.ds(6, tm), :]
    for j in range(1, DN_CONV):
        y = y + cw[j:j + 1] * pad_ref[pl.ds(6 + j, tm), :]
    y = y * _sigmoid(y)
    gm = gm_ref[...]
    q = y[:, 0:DN_W]
    k = y[:, DN_W:2 * DN_W]
    q_out[...] = q * lax.rsqrt(_group_sum(q * q, gm) + EPS) * (DN_DK ** -0.5)
    k_out[...] = k * lax.rsqrt(_group_sum(k * k, gm) + EPS)
    v_out[...] = y[:, 2 * DN_W:]

    sm = sm_ref[...]
    par = par_ref[...]
    z = sm + par[1:2]
    softplus = jnp.maximum(z, 0.0) + jnp.log1p(jnp.exp(-jnp.abs(z)))
    g = -jnp.exp(par[0:1]) * softplus
    lane = lax.broadcasted_iota(jnp.int32, sm.shape, 1)
    gb_out[...] = jnp.where(lane < 16, g, _sigmoid(sm))


def _dn_prep(proj, conv_w, par, gmat, *, b, s, tm):
    t = b * s
    nb = s // tm
    w3 = 3 * DN_W
    cb = COL_DQKV // w3
    hb = tm // 8
    last_hb = t // 8 - 1
    return pl.pallas_call(
        functools.partial(_dn_prep_kernel, nb=nb, tm=tm),
        grid=(t // tm,),
        in_specs=[pl.BlockSpec((tm, w3), lambda i: (i, cb)),
                  pl.BlockSpec((8, w3), lambda i: (jnp.maximum(i * hb - 1, 0), cb)),
                  pl.BlockSpec((8, w3), lambda i: (jnp.minimum((i + 1) * hb, last_hb), cb)),
                  pl.BlockSpec((tm, 128), lambda i: (i, COL_SMALL // 128)),
                  pl.BlockSpec((8, w3), lambda i: (0, 0)),
                  pl.BlockSpec((8, 128), lambda i: (0, 0)),
                  pl.BlockSpec((DN_W, DN_W), lambda i: (0, 0))],
        out_specs=[pl.BlockSpec((tm, DN_W), lambda i: (i, 0)),
                   pl.BlockSpec((tm, DN_W), lambda i: (i, 0)),
                   pl.BlockSpec((tm, DN_W), lambda i: (i, 0)),
                   pl.BlockSpec((tm, 128), lambda i: (i, 0))],
        out_shape=[jax.ShapeDtypeStruct((t, DN_W), F32)] * 3 + [jax.ShapeDtypeStruct((t, 128), F32)],
        scratch_shapes=[pltpu.VMEM((tm + 16, w3), F32)],
        compiler_params=_cparams(("parallel",)),
        name="dn_prep",
    )(proj, proj, proj, proj, conv_w, par, gmat)


GRP = 256
NGRP = DN_W // GRP


def _bd_rows(x_bf, bmask):
    return jnp.where(bmask, jnp.concatenate([x_bf] * (GRP // DN_CHUNK), axis=0), jnp.zeros((), BF16))


def _dn_chunk_kernel(qf_ref, kf_ref, vf_ref, gf_ref, qb_ref, kb_ref, vb_ref, gb_ref,
                     of_ref, ob_ref, s_ref, *, nch):
    c = DN_CHUNK

    @pl.when(pl.program_id(1) == 0)
    def _():
        s_ref[...] = jnp.zeros(s_ref.shape, F32)

    ri = lax.broadcasted_iota(jnp.int32, (c, GRP), 0)
    cj = lax.broadcasted_iota(jnp.int32, (c, GRP), 1) & (c - 1)
    br = lax.broadcasted_iota(jnp.int32, (GRP, GRP), 0) >> 6
    bc = lax.broadcasted_iota(jnp.int32, (GRP, GRP), 1) >> 6
    bmask = br == bc
    ident = (ri == cj).astype(F32)
    incl = (ri >= cj, ri <= cj)
    strict = (ri > cj, ri < cj)

    def pmm(a, bmat):
        return _dot(a.astype(BF16), _bd_rows(bmat.astype(BF16), bmask))

    def pmm_nt(a, bmat):
        return _dot_nt(a.astype(BF16), _bd_rows(bmat.astype(BF16), bmask))

    er = lax.broadcasted_iota(jnp.int32, (128, DN_W), 0)
    ec = lax.broadcasted_iota(jnp.int32, (128, DN_W), 1) >> 6
    tr = lax.broadcasted_iota(jnp.int32, (c, c), 0)
    tc = lax.broadcasted_iota(jnp.int32, (c, c), 1)
    mr = lax.broadcasted_iota(jnp.int32, (c, DN_W), 0)
    mj = lax.broadcasted_iota(jnp.int32, (c, DN_W), 1) & (c - 1)
    ones_cc = jnp.ones((c, c), BF16)
    tri = ((tc <= tr).astype(BF16), (tc >= tr).astype(BF16))
    mask_t = (mr <= mj, mr >= mj)
    refs = ((qf_ref, kf_ref, vf_ref, gf_ref, of_ref), (qb_ref, kb_ref, vb_ref, gb_ref, ob_ref))

    def rows(d, step):
        i = step if d == 0 else nch - 1 - step
        return slice(i * c, (i + 1) * c)

    def sum3(lhs, rhs3):
        return _dot(lhs, rhs3[0]) + _dot(lhs, rhs3[1]) + _dot(lhs, rhs3[2])

    ds = [(d, step) for step in range(nch) for d in (0, 1)]
    g3 = {u: _split3(refs[u[0]][3][rows(*u), :]) for u in ds}
    e_g = [(er == d * DN_HEADS + ec).astype(BF16) for d in (0, 1)]
    e_b = [(er == 16 + d * DN_HEADS + ec).astype(BF16) for d in (0, 1)]
    gexp = {u: _dot(g3[u][0], e_g[u[0]]) + _dot(g3[u][1], e_g[u[0]]) + _dot(g3[u][2], e_g[u[0]]) for u in ds}
    bexp = {u: _dot(g3[u][0], e_b[u[0]]) + _dot(g3[u][1], e_b[u[0]]) + _dot(g3[u][2], e_b[u[0]]) for u in ds}
    ge3 = {u: _split3(gexp[u]) for u in ds}
    gm3 = {u: _split3(jnp.where(mask_t[u[0]], gexp[u], 0.0)) for u in ds}
    gcrow = {u: sum3(tri[u[0]], ge3[u]) for u in ds}
    gccol = {u: sum3(ones_cc, gm3[u]) for u in ds}

    units = [(d, step, g) for step in range(nch) for d in (0, 1) for g in range(NGRP)]
    sl = lambda g: slice(g * GRP, (g + 1) * GRP)
    kg = {u: refs[u[0]][1][rows(u[0], u[1]), sl(u[2])] for u in units}
    qg = {u: refs[u[0]][0][rows(u[0], u[1]), sl(u[2])] for u in units}
    vg = {u: refs[u[0]][2][rows(u[0], u[1]), sl(u[2])] for u in units}
    bg = {u: bexp[u[:2]][:, sl(u[2])] for u in units}
    gc = {u: gcrow[u[:2]][:, sl(u[2])] for u in units}
    dec = {u: jnp.exp(jnp.where(incl[u[0]], gc[u] - gccol[u[:2]][:, sl(u[2])], -jnp.inf)) for u in units}
    egc = {u: jnp.exp(gc[u]) for u in units}
    kk = {u: pmm_nt(kg[u], kg[u]) for u in units}
    qk = {u: pmm_nt(qg[u], kg[u]) for u in units}
    lm = {u: jnp.where(strict[u[0]], kk[u] * bg[u] * dec[u], 0.0) for u in units}

    dd = {u: jnp.where((ri >> 3) == (cj >> 3), lm[u], 0.0) for u in units}
    d2 = {u: pmm(dd[u], dd[u]) for u in units}
    d4 = {u: pmm(d2[u], d2[u]) for u in units}
    x = {u: pmm(ident - dd[u], ident + d2[u]) for u in units}
    x = {u: pmm(x[u], ident + d4[u]) for u in units}
    for sh in (3, 4, 5):
        off = ((ri >> (sh + 1)) == (cj >> (sh + 1))) & ((ri >> sh) != (cj >> sh))
        xm = {u: pmm(x[u], jnp.where(off, lm[u], 0.0)) for u in units}
        x = {u: x[u] - pmm(xm[u], x[u]) for u in units}
    uu = {u: pmm(x[u], vg[u] * bg[u]) for u in units}
    ww = {u: pmm(x[u], kg[u] * bg[u] * egc[u]).astype(BF16) for u in units}
    att = {u: jnp.where(incl[u[0]], qk[u] * dec[u], 0.0).astype(BF16) for u in units}
    qe = {u: (qg[u] * egc[u]).astype(BF16) for u in units}
    gl = {u: (gc[u][c - 1:c] if u[0] == 0 else gc[u][0:1]) for u in units}
    kdec = {u: (kg[u] * jnp.exp(gl[u] - gc[u])).astype(BF16) for u in units}
    egl = {u: jnp.exp(gl[u]) for u in units}

    for step in range(nch):
        cur = [(d, step, g) for d in (0, 1) for g in range(NGRP)]
        st = {u: s_ref[u[0] * NGRP + u[2]] for u in cur}
        sb = {u: st[u].astype(BF16) for u in cur}
        vnew = {u: uu[u] - _dot(ww[u], sb[u]) for u in cur}
        vbd = {u: _bd_rows(vnew[u].astype(BF16), bmask) for u in cur}
        for u in cur:
            refs[u[0]][4][rows(u[0], step), sl(u[2])] = _dot(qe[u], sb[u]) + _dot(att[u], vbd[u])
        upd = {u: _dot_tn(kdec[u], vnew[u].astype(BF16)) for u in cur}
        for u in cur:
            s_ref[u[0] * NGRP + u[2]] = st[u] * egl[u] + jnp.where(bmask, upd[u], 0.0)


def _dn_chunk(qn, kn, vv, gb, *, b, s, nch):
    n = s // (DN_CHUNK * nch)
    t = b * s
    fwd = lambda bi, ci: (bi * n + ci, 0)
    bwd = lambda bi, ci: (bi * n + n - 1 - ci, 0)
    wide = lambda im: pl.BlockSpec((DN_CHUNK * nch, DN_W), im)
    small = lambda im: pl.BlockSpec((DN_CHUNK * nch, 128), im)
    return pl.pallas_call(
        functools.partial(_dn_chunk_kernel, nch=nch),
        grid=(b, n),
        in_specs=[wide(fwd), wide(fwd), wide(fwd), small(fwd), wide(bwd), wide(bwd), wide(bwd), small(bwd)],
        out_specs=[wide(fwd), wide(bwd)],
        out_shape=[jax.ShapeDtypeStruct((t, DN_W), F32)] * 2,
        scratch_shapes=[pltpu.VMEM((2 * NGRP, GRP, GRP), F32)],
        compiler_params=_cparams(("parallel", "arbitrary")),
        name="dn_chunk",
    )(qn, kn, vv, gb, qn, kn, vv, gb)


def _merge_kernel(x_ref, gates_ref, dz_ref, mq_ref, omla_ref, of_ref, ob_ref, mk_ref, mv_ref,
                  onorm_ref, gm_ref, wm_ref, wd_ref, wmem_ref, wout_ref, o_ref):
    o = of_ref[...] + ob_ref[...]
    ms = _group_sum(o * o, gm_ref[...]) * (1.0 / DN_DK)
    dz = dz_ref[...]
    o_dn = o * lax.rsqrt(ms + EPS) * onorm_ref[...] * (dz * _sigmoid(dz))

    mq = mq_ref[...]
    mk = mk_ref[0]
    mv = mv_ref[0]
    outs = []
    for h in range(MEM_HEADS):
        hs = slice(h * MEM_HD, (h + 1) * MEM_HD)
        sc = _dot_nt(mq[:, hs].astype(BF16), mk[:, hs]) * (MEM_HD ** -0.5)
        p = jnp.exp(sc - jnp.max(sc, axis=-1, keepdims=True))
        p = p / jnp.sum(p, axis=-1, keepdims=True)
        outs.append(_dot(p.astype(BF16), mv[:, hs]))
    o_mem = jnp.concatenate(outs, axis=-1)

    gates = gates_ref[...]
    merged = (_sigmoid(gates[:, 0:D_MODEL]) * _dot(omla_ref[...], wm_ref[...])
              + _sigmoid(gates[:, D_MODEL:2 * D_MODEL]) * _dot(o_dn.astype(BF16), wd_ref[...])
              + _sigmoid(gates[:, 2 * D_MODEL:]) * _dot(o_mem.astype(BF16), wmem_ref[...]))
    o_ref[...] = x_ref[...] + _dot(merged.astype(BF16), wout_ref[...])


def _merge(x, proj, omla, o_f, o_b, memkv, onorm, gmat, wm, wd, wmem, wout, *, b, s, tm):
    t = b * s
    nb = s // tm
    mt = memkv.shape[1]
    const = lambda shape: pl.BlockSpec(shape, lambda i: (0,) * len(shape))
    return pl.pallas_call(
        _merge_kernel,
        grid=(t // tm,),
        in_specs=[pl.BlockSpec((tm, D_MODEL), lambda i: (i, 0)),
                  pl.BlockSpec((tm, 3 * D_MODEL), lambda i: (i, COL_GATES // (3 * D_MODEL))),
                  pl.BlockSpec((tm, DN_W), lambda i: (i, COL_DZ // DN_W)),
                  pl.BlockSpec((tm, MEM_W), lambda i: (i, COL_MQ // MEM_W)),
                  pl.BlockSpec((tm, DN_W), lambda i: (i, 0)),
                  pl.BlockSpec((tm, DN_W), lambda i: (i, 0)),
                  pl.BlockSpec((tm, DN_W), lambda i: (i, 0)),
                  pl.BlockSpec((1, mt, MEM_W), lambda i: (i // nb, 0, 0)),
                  pl.BlockSpec((1, mt, MEM_W), lambda i: (i // nb, 0, 1)),
                  const((1, DN_W)), const((DN_W, DN_W)),
                  const((DN_W, D_MODEL)), const((DN_W, D_MODEL)), const((MEM_W, D_MODEL)),
                  const((D_MODEL, D_MODEL))],
        out_specs=pl.BlockSpec((tm, D_MODEL), lambda i: (i, 0)),
        out_shape=jax.ShapeDtypeStruct((t, D_MODEL), F32),
        compiler_params=_cparams(("parallel",)),
        name="merge",
    )(x, proj, proj, proj, omla, o_f, o_b, memkv, memkv, onorm, gmat, wm, wd, wmem, wout)


def _ffn_kernel(x_ref, nw_ref, wgu_ref, wd_ref, fw_ref, o_ref, *, nf, final_norm):
    x = x_ref[...]
    h = _rms(x, nw_ref[...]).astype(BF16)
    tf = D_FF // nf
    y = x
    for c in range(nf):
        gate = _dot(h, wgu_ref[:, c * tf:(c + 1) * tf])
        up = _dot(h, wgu_ref[:, D_FF + c * tf:D_FF + (c + 1) * tf])
        act = (gate * _sigmoid(gate) * up).astype(BF16)
        y = y + _dot(act, wd_ref[c * tf:(c + 1) * tf, :])
    if final_norm:
        y = _rms(y, fw_ref[...])
    o_ref[...] = y


def _ffn(x, nw, wgu, wd, fw, *, tm, nf, final_norm):
    t = x.shape[0]
    resident = lambda shape: pl.BlockSpec(shape, lambda i: (0, 0), pipeline_mode=pl.Buffered(1))
    return pl.pallas_call(
        functools.partial(_ffn_kernel, nf=nf, final_norm=final_norm),
        grid=(t // tm,),
        in_specs=[pl.BlockSpec((tm, D_MODEL), lambda i: (i, 0)),
                  pl.BlockSpec((1, D_MODEL), lambda i: (0, 0)),
                  resident((D_MODEL, 2 * D_FF)),
                  resident((D_FF, D_MODEL)),
                  pl.BlockSpec((1, D_MODEL), lambda i: (0, 0))],
        out_specs=pl.BlockSpec((tm, D_MODEL), lambda i: (i, 0)),
        out_shape=jax.ShapeDtypeStruct((t, D_MODEL), F32),
        compiler_params=_cparams(("parallel",)),
        name="ffn",
    )(x, nw, wgu, wd, fw)


def _pick(n, pref):
    return pref if n % pref == 0 else n


def _rope_tables(positions):
    b, s = positions.shape
    inv_freq = 1.0 / (ROPE_THETA ** (jnp.arange(0, MLA_ROPE, 2, dtype=F32) / MLA_ROPE))
    ang = positions.astype(F32)[..., None] * inv_freq
    cos = jnp.cos(ang).reshape(b * s, MLA_ROPE // 2)
    sin = jnp.sin(ang).reshape(b * s, MLA_ROPE // 2)
    z64 = jnp.zeros((b * s, MLA_NOPE), F32)
    z32 = jnp.zeros((b * s, 32), F32)
    ck = jnp.concatenate([z64, cos, cos, z32], axis=1)
    sk = jnp.concatenate([z64, sin, sin, z32], axis=1)
    qscale = (MLA_QK ** -0.5) * LOG2E
    cq = jnp.concatenate([jnp.ones_like(z64), cos, cos, z32], axis=1) * qscale
    sq = sk * qscale
    to_t = lambda a: a.reshape(b, s, 128).transpose(0, 2, 1)
    return ck, sk, to_t(cq), to_t(sq)


def _layer_weights(w_in, w_uq, w_ukv):
    d = w_in.shape[0]
    w_in = w_in.astype(BF16)
    z = lambda n: jnp.zeros((d, n), BF16)
    kr = w_in[:, 512:544]
    w_re = jnp.concatenate([
        w_in[:, 3136:6208], w_in[:, 544:2080], w_in[:, 2080:2592], w_in[:, 2624:3136],
        w_in[:, 0:256], w_in[:, 256:512],
        w_in[:, 2592:2624], z(32), kr, z(32),
        z(64), -kr[:, 16:32], kr[:, 0:16], z(32)], axis=1)

    uq = w_uq.reshape(MLA_Q_LORA, MLA_HEADS, MLA_QK)
    zq = lambda n: jnp.zeros((MLA_Q_LORA, MLA_HEADS, n), w_uq.dtype)
    x1, x2 = uq[..., 64:80], uq[..., 80:96]
    wq1 = jnp.concatenate([uq[..., :64], x1, x2, zq(32)], axis=-1).reshape(MLA_Q_LORA, 1024)
    wq2 = jnp.concatenate([zq(64), -x2, x1, zq(32)], axis=-1).reshape(MLA_Q_LORA, 1024)
    ukv = w_ukv.reshape(MLA_KV_LORA, MLA_HEADS, MLA_NOPE + MLA_V)
    wk = jnp.concatenate([ukv[..., :64], jnp.zeros((MLA_KV_LORA, MLA_HEADS, 64), w_ukv.dtype)],
                         axis=-1).reshape(MLA_KV_LORA, 1024)
    wv = ukv[..., 64:].reshape(MLA_KV_LORA, MLA_HEADS * MLA_V)
    return w_re, wq1.T.astype(BF16), wq2.T.astype(BF16), wk.astype(BF16), wv.T.astype(BF16)


def kernel(x, mem, positions, norm_mix, w_in, mla_q_norm, mla_w_uq, mla_kv_norm, mla_w_ukv, dn_conv, dn_a_log, dn_dt_bias, dn_out_norm, mem_norm, mem_w_kv, w_branch_mla, w_branch_dn, w_branch_mem, w_out, norm_ffn, ffn_w_gate_up, ffn_w_down, final_norm):
    b, s, d = x.shape
    t = b * s
    depth = w_in.shape[0]
    mt = mem.shape[1]
    ck, sk, cqt, sqt = _rope_tables(positions)
    gi = jnp.arange(DN_W) // DN_DK
    gmat = (gi[:, None] == gi[None, :]).astype(BF16)
    xf = x.reshape(t, d)
    memf = mem.reshape(b * mt, d)
    tm = _pick(s, 512)
    for l in range(depth):
        w_re, wq1t, wq2t, wk, wvt = _layer_weights(w_in[l], mla_w_uq[l], mla_w_ukv[l])
        proj = _norm_matmul(xf, norm_mix[l][None], w_re, tm=_pick(t, 1024), tn=1280, out_dtype=F32)

        qt, kk, vt = _mla_prep(proj, ck, sk, cqt, sqt, mla_q_norm[l][None], mla_kv_norm[l][None],
                               wq1t, wq2t, wk, wvt, b=b, s=s, tm=tm)
        o_mla = _flash(qt, kk, vt, b=b, s=s, tq=_pick(s, 512), tk=_pick(s, 256), unroll=8)

        conv_w = jnp.concatenate([dn_conv[l], jnp.zeros((8 - DN_CONV, 3 * DN_W), F32)], axis=0)
        pad16 = lambda v: jnp.concatenate([v.reshape(-1), jnp.zeros((128 - 2 * DN_HEADS,), F32)])
        par = jnp.concatenate([pad16(dn_a_log[l])[None], pad16(dn_dt_bias[l])[None],
                               jnp.zeros((6, 128), F32)], axis=0)
        qn, kn, vv, gb = _dn_prep(proj, conv_w, par, gmat, b=b, s=s, tm=tm)
        o_f, o_b = _dn_chunk(qn, kn, vv, gb, b=b, s=s, nch=4)
        n_chunks = s // DN_CHUNK
        perm = lambda a: a.reshape(b, n_chunks, DN_CHUNK, DN_W).swapaxes(1, 2).reshape(t, DN_W)
        o_f, o_b = perm(o_f), perm(o_b)

        memkv = _norm_matmul(memf, mem_norm[l][None], mem_w_kv[l].astype(BF16),
                             tm=b * mt, tn=512, out_dtype=BF16).reshape(b, mt, 2 * MEM_W)
        onorm = jnp.tile(dn_out_norm[l], DN_HEADS)[None]
        xf = _merge(xf, proj, o_mla, o_f, o_b, memkv, onorm, gmat,
                    w_branch_mla[l].astype(BF16), w_branch_dn[l].astype(BF16),
                    w_branch_mem[l].astype(BF16), w_out[l].astype(BF16), b=b, s=s, tm=tm)
        xf = _ffn(xf, norm_ffn[l][None], ffn_w_gate_up[l].astype(BF16), ffn_w_down[l].astype(BF16),
                  final_norm[None], tm=tm, nf=2, final_norm=(l == depth - 1))
    return xf.reshape(b, s, d)
```

```python
import functools

import jax
import jax.numpy as jnp
from jax import lax
from jax.experimental import pallas as pl
from jax.experimental.pallas import tpu as pltpu

F32 = jnp.float32
BF16 = jnp.bfloat16

D_MODEL = 1024
EPS = 1e-6
MLA_HEADS = 8
MLA_Q_LORA = 256
MLA_KV_LORA = 256
MLA_NOPE = 64
MLA_ROPE = 32
MLA_V = 64
MLA_QK = MLA_NOPE + MLA_ROPE
ROPE_THETA = 10000.0
DN_HEADS = 8
DN_DK = 64
DN_W = DN_HEADS * DN_DK
DN_CONV = 5
DN_CHUNK = 64
MEM_HEADS = 4
MEM_HD = 128
MEM_W = MEM_HEADS * MEM_HD
D_FF = 2816

COL_GATES = 0
COL_DQKV = 3072
COL_DZ = 4608
COL_MQ = 5120
COL_CQ = 5632
COL_CKV = 5888
COL_SMALL = 6144
COL_SMALL2 = 6272
IN_COLS_P = 6400

HEAD_SLAB = 128
VT_ROWS = 80
LOG2E = 1.4426950408889634
VMEM_LIMIT = 48 * 1024 * 1024


def _cparams(sem):
    return pltpu.CompilerParams(dimension_semantics=sem, vmem_limit_bytes=VMEM_LIMIT)


def _sigmoid(x):
    return 1.0 / (1.0 + jnp.exp(-x))


def _dot(a, b):
    return jnp.dot(a, b, preferred_element_type=F32)


def _dot_nt(a, b):
    return lax.dot_general(a, b, (((1,), (1,)), ((), ())), preferred_element_type=F32)


def _dot_tn(a, b):
    return lax.dot_general(a, b, (((0,), (0,)), ((), ())), preferred_element_type=F32)


def _split3(x):
    x1 = x.astype(BF16)
    r1 = x - x1.astype(F32)
    x2 = r1.astype(BF16)
    x3 = (r1 - x2.astype(F32)).astype(BF16)
    return x1, x2, x3


def _rms(x, w):
    return x * lax.rsqrt(jnp.mean(x * x, axis=-1, keepdims=True) + EPS) * w


def _norm_matmul_kernel(x_ref, g_ref, w_ref, o_ref, *rest, tail):
    h_ref = rest[-1]

    @pl.when(pl.program_id(1) == 0)
    def _():
        h_ref[...] = _rms(x_ref[...], g_ref[...]).astype(BF16)

    acc = _dot(h_ref[...], w_ref[...])
    o_ref[...] = acc.astype(o_ref.dtype)
    if tail:
        @pl.when(pl.program_id(1) == pl.num_programs(1) - 1)
        def _():
            rest[0][...] = acc[:, acc.shape[1] - tail:]


def _norm_matmul(x, g, w, *, tm, tn, tail=0):
    t, d = x.shape
    n = w.shape[1]
    out_specs = [pl.BlockSpec((tm, tn), lambda i, j: (i, j))]
    out_shape = [jax.ShapeDtypeStruct((t, n), BF16)]
    if tail:
        out_specs.append(pl.BlockSpec((tm, tail), lambda i, j: (i, 0)))
        out_shape.append(jax.ShapeDtypeStruct((t, tail), F32))
    return pl.pallas_call(
        functools.partial(_norm_matmul_kernel, tail=tail),
        grid=(t // tm, n // tn),
        in_specs=[pl.BlockSpec((tm, d), lambda i, j: (i, 0)),
                  pl.BlockSpec((1, d), lambda i, j: (0, 0)),
                  pl.BlockSpec((d, tn), lambda i, j: (0, j))],
        out_specs=out_specs,
        out_shape=out_shape,
        scratch_shapes=[pltpu.VMEM((tm, d), BF16)],
        compiler_params=_cparams(("parallel", "arbitrary")),
        name="norm_matmul",
    )(x, g, w)


def _mla_prep_kernel(cq_ref, ckv_ref, sm_ref, ck_ref, sk_ref, cqt_ref, sqt_ref, qn_ref, kvn_ref,
                     wq1t_ref, wq2t_ref, wk_ref, wvt_ref, qt_out, k_out, vt_out):
    qn = _rms(cq_ref[...].astype(F32), qn_ref[...]).astype(BF16)
    q1t = _dot_nt(wq1t_ref[...], qn)
    q2t = _dot_nt(wq2t_ref[...], qn)
    ct = cqt_ref[0]
    st = sqt_ref[0]
    for h in range(MLA_HEADS):
        hs = slice(h * HEAD_SLAB, (h + 1) * HEAD_SLAB)
        qt_out[0, hs, :] = (q1t[hs] * ct + q2t[hs] * st).astype(BF16)

    kvn = _rms(ckv_ref[...].astype(F32), kvn_ref[...]).astype(BF16)
    kk = _dot(kvn, wk_ref[...])
    sm = sm_ref[...]
    kr = sm[:, :128] * ck_ref[...] + sm[:, 128:] * sk_ref[...]
    for h in range(MLA_HEADS):
        hs = slice(h * HEAD_SLAB, (h + 1) * HEAD_SLAB)
        k_out[:, hs] = (kk[:, hs] + kr).astype(BF16)

    vt = _dot_nt(wvt_ref[...], kvn)
    tm = vt.shape[1]
    ones = jnp.ones((VT_ROWS - MLA_V, tm), BF16)
    for h in range(MLA_HEADS):
        vt_out[0, h * VT_ROWS:h * VT_ROWS + MLA_V, :] = vt[h * MLA_V:(h + 1) * MLA_V].astype(BF16)
        vt_out[0, h * VT_ROWS + MLA_V:(h + 1) * VT_ROWS, :] = ones


def _mla_prep(proj, small, ck, sk, cqt, sqt, qnorm, kvnorm, wq1t, wq2t, wk, wvt, *, b, s, tm):
    nb = s // tm
    t = b * s
    row = lambda bi, i: bi * nb + i
    const = lambda shape: pl.BlockSpec(shape, lambda bi, i: (0,) * len(shape))
    return pl.pallas_call(
        _mla_prep_kernel,
        grid=(b, nb),
        in_specs=[pl.BlockSpec((tm, 256), lambda bi, i: (row(bi, i), COL_CQ // 256)),
                  pl.BlockSpec((tm, 256), lambda bi, i: (row(bi, i), COL_CKV // 256)),
                  pl.BlockSpec((tm, 256), lambda bi, i: (row(bi, i), 0)),
                  pl.BlockSpec((tm, 128), lambda bi, i: (row(bi, i), 0)),
                  pl.BlockSpec((tm, 128), lambda bi, i: (row(bi, i), 0)),
                  pl.BlockSpec((1, 128, tm), lambda bi, i: (bi, 0, i)),
                  pl.BlockSpec((1, 128, tm), lambda bi, i: (bi, 0, i)),
                  const((1, 256)), const((1, 256)),
                  const((1024, 256)), const((1024, 256)), const((256, 1024)), const((512, 256))],
        out_specs=[pl.BlockSpec((1, 1024, tm), lambda bi, i: (bi, 0, i)),
                   pl.BlockSpec((tm, 1024), lambda bi, i: (row(bi, i), 0)),
                   pl.BlockSpec((1, MLA_HEADS * VT_ROWS, tm), lambda bi, i: (bi, 0, i))],
        out_shape=[jax.ShapeDtypeStruct((b, 1024, s), BF16),
                   jax.ShapeDtypeStruct((t, 1024), BF16),
                   jax.ShapeDtypeStruct((b, MLA_HEADS * VT_ROWS, s), BF16)],
        compiler_params=_cparams(("parallel", "parallel")),
        name="mla_prep",
    )(proj, proj, small, ck, sk, cqt, sqt, qnorm, kvnorm, wq1t, wq2t, wk, wvt)


def _chunk_start(j, tk):
    return j * tk if isinstance(j, int) else pl.multiple_of(j * tk, tk)


def _flash_kernel(qt_ref, k_ref, vt_ref, o_ref, m_ref, acc_ref, s_buf, p_buf, a_buf, *, tq, tk, nq, nk, unroll):
    total = nq * nk
    lognk = nk.bit_length() - 1
    m_ref[...] = jnp.full(m_ref.shape, -jnp.inf, F32)
    acc_ref[...] = jnp.zeros(acc_ref.shape, F32)

    def split(n):
        if isinstance(n, int):
            return n // nk, n % nk
        return n >> lognk, n & (nk - 1)

    def scores(n, par):
        qi, j = split(n)
        q0 = _chunk_start(qi, tq)
        r0 = _chunk_start(j, tk)
        for h in range(2):
            hs = slice(h * HEAD_SLAB, (h + 1) * HEAD_SLAB)
            s_buf[par, h] = _dot(k_ref[pl.ds(r0, tk), hs],
                                 qt_ref[0, hs, pl.ds(q0, tq)]).astype(s_buf.dtype)

    def softmax(n, par):
        _, j = split(n)
        for h in range(2):
            st = s_buf[par, h]
            m_old = jnp.where(j == 0, -jnp.inf, m_ref[h])
            m_new = jnp.maximum(m_old, jnp.max(st, axis=0, keepdims=True).astype(F32))
            a_buf[par, h] = jnp.exp2(m_old - m_new)
            p_buf[par, h] = jnp.exp2(st - m_new.astype(st.dtype)).astype(BF16)
            m_ref[h] = m_new

    def pv(n, par):
        _, j = split(n)
        r0 = _chunk_start(j, tk)
        for h in range(2):
            vc = vt_ref[0, h * VT_ROWS:(h + 1) * VT_ROWS, pl.ds(r0, tk)]
            acc_ref[h] = a_buf[par, h] * acc_ref[h] + _dot(vc, p_buf[par, h])

    def finalize(qi):
        outs = []
        for h in range(2):
            a = acc_ref[h]
            outs.append(a[0:MLA_V] / a[MLA_V:MLA_V + 1])
        ot = jnp.concatenate(outs, axis=0)
        o_ref[pl.ds(_chunk_start(qi, tq), tq), :] = ot.T.astype(o_ref.dtype)

    scores(0, 0)
    scores(1, 1)
    softmax(0, 0)
    trips = (total - 2) // unroll
    assert trips == 0 or (nk % unroll == 0 and nk == 1 << lognk and unroll % 2 == 0)

    def body(i, carry):
        for u in range(unroll):
            n = unroll * i + u
            pv(n, u % 2)
            softmax(n + 1, (u + 1) % 2)
            scores(n + 2, u % 2)
        n_last = unroll * i + unroll - 1

        @pl.when((n_last & (nk - 1)) == nk - 1)
        def _():
            finalize(n_last >> lognk)
        return carry

    lax.fori_loop(0, trips, body, 0)
    for n in range(trips * unroll, total):
        pv(n, n % 2)
        if n + 1 < total:
            softmax(n + 1, (n + 1) % 2)
        if n + 2 < total:
            scores(n + 2, n % 2)
        if n % nk == nk - 1:
            finalize(n // nk)


def _flash(qt, k, vt, *, b, s, tq, tk, unroll=4, s_dtype=BF16):
    nq = s // tq
    nk = s // tk
    hp = MLA_HEADS // 2
    return pl.pallas_call(
        functools.partial(_flash_kernel, tq=tq, tk=tk, nq=nq, nk=nk, unroll=unroll),
        grid=(b, hp),
        in_specs=[pl.BlockSpec((1, 2 * HEAD_SLAB, s), lambda bi, p: (bi, p, 0)),
                  pl.BlockSpec((s, 2 * HEAD_SLAB), lambda bi, p: (bi, p)),
                  pl.BlockSpec((1, 2 * VT_ROWS, s), lambda bi, p: (bi, p, 0))],
        out_specs=pl.BlockSpec((s, 2 * MLA_V), lambda bi, p: (bi, p)),
        out_shape=jax.ShapeDtypeStruct((b * s, MLA_HEADS * MLA_V), BF16),
        scratch_shapes=[pltpu.VMEM((2, 1, tq), F32), pltpu.VMEM((2, VT_ROWS, tq), F32),
                        pltpu.VMEM((2, 2, tk, tq), s_dtype), pltpu.VMEM((2, 2, tk, tq), BF16),
                        pltpu.VMEM((2, 2, 1, tq), F32)],
        compiler_params=_cparams(("parallel", "arbitrary")),
        name="mla_flash",
    )(qt, k, vt)


def _group_sum(z, gmat):
    z1 = z.astype(BF16)
    z2 = (z - z1.astype(F32)).astype(BF16)
    return _dot(z1, gmat) + _dot(z2, gmat)


HALO = 16


def _dn_prep_kernel(x_ref, xp_ref, xn_ref, sm_ref, cw_ref, par_ref, gm_ref,
                    q_out, k_out, v_out, gb_out, pad_ref, *, nb, tm):
    i = pl.program_id(0)
    first = (i % nb) == 0
    last = (i % nb) == nb - 1
    pad_ref[0:HALO, :] = jnp.where(first, 0.0, xp_ref[...].astype(F32))
    pad_ref[HALO:HALO + tm, :] = x_ref[...].astype(F32)
    pad_ref[HALO + tm:2 * HALO + tm, :] = jnp.where(last, 0.0, xn_ref[...].astype(F32))
    cw = cw_ref[...]
    off = HALO - DN_CONV // 2
    y = cw[0:1] * pad_ref[pl.ds(off, tm), :]
    for j in range(1, DN_CONV):
        y = y + cw[j:j + 1] * pad_ref[pl.ds(off + j, tm), :]
    y = y * _sigmoid(y)
    gm = gm_ref[...]
    q = y[:, 0:DN_W]
    k = y[:, DN_W:2 * DN_W]
    q_out[...] = q * lax.rsqrt(_group_sum(q * q, gm) + EPS) * (DN_DK ** -0.5)
    k_out[...] = k * lax.rsqrt(_group_sum(k * k, gm) + EPS)
    v_out[...] = y[:, 2 * DN_W:]

    sm = sm_ref[...]
    par = par_ref[...]
    z = sm + par[1:2]
    softplus = jnp.maximum(z, 0.0) + jnp.log1p(jnp.exp(-jnp.abs(z)))
    g = -jnp.exp(par[0:1]) * softplus
    lane = lax.broadcasted_iota(jnp.int32, sm.shape, 1)
    gb_out[...] = jnp.where(lane < 16, g, _sigmoid(sm))


def _dn_prep(proj, small, conv_w, par, gmat, *, b, s, tm):
    t = b * s
    nb = s // tm
    w3 = 3 * DN_W
    cb = COL_DQKV // w3
    hb = tm // HALO
    last_hb = t // HALO - 1
    return pl.pallas_call(
        functools.partial(_dn_prep_kernel, nb=nb, tm=tm),
        grid=(t // tm,),
        in_specs=[pl.BlockSpec((tm, w3), lambda i: (i, cb)),
                  pl.BlockSpec((HALO, w3), lambda i: (jnp.maximum(i * hb - 1, 0), cb)),
                  pl.BlockSpec((HALO, w3), lambda i: (jnp.minimum((i + 1) * hb, last_hb), cb)),
                  pl.BlockSpec((tm, 128), lambda i: (i, 0)),
                  pl.BlockSpec((8, w3), lambda i: (0, 0)),
                  pl.BlockSpec((8, 128), lambda i: (0, 0)),
                  pl.BlockSpec((DN_W, DN_W), lambda i: (0, 0))],
        out_specs=[pl.BlockSpec((tm, DN_W), lambda i: (i, 0)),
                   pl.BlockSpec((tm, DN_W), lambda i: (i, 0)),
                   pl.BlockSpec((tm, DN_W), lambda i: (i, 0)),
                   pl.BlockSpec((tm, 128), lambda i: (i, 0))],
        out_shape=[jax.ShapeDtypeStruct((t, DN_W), F32)] * 3 + [jax.ShapeDtypeStruct((t, 128), F32)],
        scratch_shapes=[pltpu.VMEM((tm + 2 * HALO, w3), F32)],
        compiler_params=_cparams(("parallel",)),
        name="dn_prep",
    )(proj, proj, proj, small, conv_w, par, gmat)


GRP = 256
NGRP = DN_W // GRP


def _bd_rows(x_bf, bmask):
    return jnp.where(bmask, jnp.concatenate([x_bf] * (GRP // DN_CHUNK), axis=0), jnp.zeros((), BF16))


def _dn_chunk_kernel(qf_ref, kf_ref, vf_ref, gf_ref, qb_ref, kb_ref, vb_ref, gb_ref,
                     of_ref, ob_ref, s_ref, *, nch):
    c = DN_CHUNK

    @pl.when(pl.program_id(1) == 0)
    def _():
        s_ref[...] = jnp.zeros(s_ref.shape, F32)

    ri = lax.broadcasted_iota(jnp.int32, (c, GRP), 0)
    cj = lax.broadcasted_iota(jnp.int32, (c, GRP), 1) & (c - 1)
    br = lax.broadcasted_iota(jnp.int32, (GRP, GRP), 0) >> 6
    bc = lax.broadcasted_iota(jnp.int32, (GRP, GRP), 1) >> 6
    bmask = br == bc
    ident = (ri == cj).astype(F32)
    incl = (ri >= cj, ri <= cj)
    strict = (ri > cj, ri < cj)

    def pmm(a, bmat):
        return _dot(a.astype(BF16), _bd_rows(bmat.astype(BF16), bmask))

    def pmm_nt(a, bmat):
        return _dot_nt(a.astype(BF16), _bd_rows(bmat.astype(BF16), bmask))

    er = lax.broadcasted_iota(jnp.int32, (128, DN_W), 0)
    ec = lax.broadcasted_iota(jnp.int32, (128, DN_W), 1) >> 6
    tr = lax.broadcasted_iota(jnp.int32, (c, c), 0)
    tc = lax.broadcasted_iota(jnp.int32, (c, c), 1)
    mr = lax.broadcasted_iota(jnp.int32, (c, DN_W), 0)
    mj = lax.broadcasted_iota(jnp.int32, (c, DN_W), 1) & (c - 1)
    ones_cc = jnp.ones((c, c), BF16)
    tri = ((tc <= tr).astype(BF16), (tc >= tr).astype(BF16))
    mask_t = (mr <= mj, mr >= mj)
    refs = ((qf_ref, kf_ref, vf_ref, gf_ref, of_ref), (qb_ref, kb_ref, vb_ref, gb_ref, ob_ref))

    def rows(d, step):
        i = step if d == 0 else nch - 1 - step
        return slice(i * c, (i + 1) * c)

    def sum3(lhs, rhs3):
        return _dot(lhs, rhs3[0]) + _dot(lhs, rhs3[1]) + _dot(lhs, rhs3[2])

    ds = [(d, step) for step in range(nch) for d in (0, 1)]
    g3 = {u: _split3(refs[u[0]][3][rows(*u), :]) for u in ds}
    e_g = [(er == d * DN_HEADS + ec).astype(BF16) for d in (0, 1)]
    e_b = [(er == 16 + d * DN_HEADS + ec).astype(BF16) for d in (0, 1)]
    gexp = {u: _dot(g3[u][0], e_g[u[0]]) + _dot(g3[u][1], e_g[u[0]]) + _dot(g3[u][2], e_g[u[0]]) for u in ds}
    bexp = {u: _dot(g3[u][0], e_b[u[0]]) + _dot(g3[u][1], e_b[u[0]]) + _dot(g3[u][2], e_b[u[0]]) for u in ds}
    ge3 = {u: _split3(gexp[u]) for u in ds}
    gm3 = {u: _split3(jnp.where(mask_t[u[0]], gexp[u], 0.0)) for u in ds}
    gcrow = {u: sum3(tri[u[0]], ge3[u]) for u in ds}
    gccol = {u: sum3(ones_cc, gm3[u]) for u in ds}

    units = [(d, step, g) for step in range(nch) for d in (0, 1) for g in range(NGRP)]
    sl = lambda g: slice(g * GRP, (g + 1) * GRP)
    kg = {u: refs[u[0]][1][rows(u[0], u[1]), sl(u[2])] for u in units}
    qg = {u: refs[u[0]][0][rows(u[0], u[1]), sl(u[2])] for u in units}
    vg = {u: refs[u[0]][2][rows(u[0], u[1]), sl(u[2])] for u in units}
    bg = {u: bexp[u[:2]][:, sl(u[2])] for u in units}
    gc = {u: gcrow[u[:2]][:, sl(u[2])] for u in units}
    dec = {u: jnp.exp(jnp.where(incl[u[0]], gc[u] - gccol[u[:2]][:, sl(u[2])], -jnp.inf)) for u in units}
    egc = {u: jnp.exp(gc[u]) for u in units}
    kk = {u: pmm_nt(kg[u], kg[u]) for u in units}
    qk = {u: pmm_nt(qg[u], kg[u]) for u in units}
    lm = {u: jnp.where(strict[u[0]], kk[u] * bg[u] * dec[u], 0.0) for u in units}

    dd = {u: jnp.where((ri >> 3) == (cj >> 3), lm[u], 0.0) for u in units}
    d2 = {u: pmm(dd[u], dd[u]) for u in units}
    d4 = {u: pmm(d2[u], d2[u]) for u in units}
    x = {u: pmm(ident - dd[u], ident + d2[u]) for u in units}
    x = {u: pmm(x[u], ident + d4[u]) for u in units}
    for sh in (3, 4, 5):
        off = ((ri >> (sh + 1)) == (cj >> (sh + 1))) & ((ri >> sh) != (cj >> sh))
        xm = {u: pmm(x[u], jnp.where(off, lm[u], 0.0)) for u in units}
        x = {u: x[u] - pmm(xm[u], x[u]) for u in units}
    uu = {u: pmm(x[u], vg[u] * bg[u]) for u in units}
    ww = {u: pmm(x[u], kg[u] * bg[u] * egc[u]).astype(BF16) for u in units}
    att = {u: jnp.where(incl[u[0]], qk[u] * dec[u], 0.0).astype(BF16) for u in units}
    qe = {u: (qg[u] * egc[u]).astype(BF16) for u in units}
    gl = {u: (gc[u][c - 1:c] if u[0] == 0 else gc[u][0:1]) for u in units}
    kdec = {u: (kg[u] * jnp.exp(gl[u] - gc[u])).astype(BF16) for u in units}
    egl = {u: jnp.exp(gl[u]) for u in units}

    for step in range(nch):
        cur = [(d, step, g) for d in (0, 1) for g in range(NGRP)]
        st = {u: s_ref[u[0] * NGRP + u[2]] for u in cur}
        sb = {u: st[u].astype(BF16) for u in cur}
        vnew = {u: uu[u] - _dot(ww[u], sb[u]) for u in cur}
        vbd = {u: _bd_rows(vnew[u].astype(BF16), bmask) for u in cur}
        for u in cur:
            refs[u[0]][4][rows(u[0], step), sl(u[2])] = _dot(qe[u], sb[u]) + _dot(att[u], vbd[u])
        upd = {u: _dot_tn(kdec[u], vnew[u].astype(BF16)) for u in cur}
        for u in cur:
            s_ref[u[0] * NGRP + u[2]] = st[u] * egl[u] + jnp.where(bmask, upd[u], 0.0)


def _dn_chunk(qn, kn, vv, gb, *, b, s, nch):
    n = s // (DN_CHUNK * nch)
    t = b * s
    fwd = lambda bi, ci: (bi * n + ci, 0)
    bwd = lambda bi, ci: (bi * n + n - 1 - ci, 0)
    wide = lambda im: pl.BlockSpec((DN_CHUNK * nch, DN_W), im)
    small = lambda im: pl.BlockSpec((DN_CHUNK * nch, 128), im)
    return pl.pallas_call(
        functools.partial(_dn_chunk_kernel, nch=nch),
        grid=(b, n),
        in_specs=[wide(fwd), wide(fwd), wide(fwd), small(fwd), wide(bwd), wide(bwd), wide(bwd), small(bwd)],
        out_specs=[wide(fwd), wide(bwd)],
        out_shape=[jax.ShapeDtypeStruct((t, DN_W), F32)] * 2,
        scratch_shapes=[pltpu.VMEM((2 * NGRP, GRP, GRP), F32)],
        compiler_params=_cparams(("parallel", "arbitrary")),
        name="dn_chunk",
    )(qn, kn, vv, gb, qn, kn, vv, gb)


def _merge_kernel(x_ref, gates_ref, dz_ref, mq_ref, omla_ref, of_ref, ob_ref, mk_ref, mv_ref,
                  onorm_ref, gm_ref, wm_ref, wd_ref, wmem_ref, wout_ref, o_ref):
    o = of_ref[...] + ob_ref[...]
    ms = _group_sum(o * o, gm_ref[...]) * (1.0 / DN_DK)
    dz = dz_ref[...].astype(F32)
    o_dn = o * lax.rsqrt(ms + EPS) * onorm_ref[...] * (dz * _sigmoid(dz))

    mq = mq_ref[...]
    mk = mk_ref[0]
    mv = mv_ref[0]
    outs = []
    for h in range(MEM_HEADS):
        hs = slice(h * MEM_HD, (h + 1) * MEM_HD)
        sc = _dot_nt(mq[:, hs], mk[:, hs]) * (MEM_HD ** -0.5)
        p = jnp.exp(sc - jnp.max(sc, axis=-1, keepdims=True))
        p = p / jnp.sum(p, axis=-1, keepdims=True)
        outs.append(_dot(p.astype(BF16), mv[:, hs]))
    o_mem = jnp.concatenate(outs, axis=-1)

    gates = gates_ref[...].astype(F32)
    merged = (_sigmoid(gates[:, 0:D_MODEL]) * _dot(omla_ref[...], wm_ref[...])
              + _sigmoid(gates[:, D_MODEL:2 * D_MODEL]) * _dot(o_dn.astype(BF16), wd_ref[...])
              + _sigmoid(gates[:, 2 * D_MODEL:]) * _dot(o_mem.astype(BF16), wmem_ref[...]))
    o_ref[...] = x_ref[...] + _dot(merged.astype(BF16), wout_ref[...])


def _merge(x, proj, omla, o_f, o_b, memkv, onorm, gmat, wm, wd, wmem, wout, *, b, s, tm):
    t = b * s
    nb = s // tm
    mt = memkv.shape[1]
    const = lambda shape: pl.BlockSpec(shape, lambda i: (0,) * len(shape))
    return pl.pallas_call(
        _merge_kernel,
        grid=(t // tm,),
        in_specs=[pl.BlockSpec((tm, D_MODEL), lambda i: (i, 0)),
                  pl.BlockSpec((tm, 3 * D_MODEL), lambda i: (i, COL_GATES // (3 * D_MODEL))),
                  pl.BlockSpec((tm, DN_W), lambda i: (i, COL_DZ // DN_W)),
                  pl.BlockSpec((tm, MEM_W), lambda i: (i, COL_MQ // MEM_W)),
                  pl.BlockSpec((tm, DN_W), lambda i: (i, 0)),
                  pl.BlockSpec((tm, DN_W), lambda i: (i, 0)),
                  pl.BlockSpec((tm, DN_W), lambda i: (i, 0)),
                  pl.BlockSpec((1, mt, MEM_W), lambda i: (i // nb, 0, 0)),
                  pl.BlockSpec((1, mt, MEM_W), lambda i: (i // nb, 0, 1)),
                  const((1, DN_W)), const((DN_W, DN_W)),
                  const((DN_W, D_MODEL)), const((DN_W, D_MODEL)), const((MEM_W, D_MODEL)),
                  const((D_MODEL, D_MODEL))],
        out_specs=pl.BlockSpec((tm, D_MODEL), lambda i: (i, 0)),
        out_shape=jax.ShapeDtypeStruct((t, D_MODEL), F32),
        compiler_params=_cparams(("parallel",)),
        name="merge",
    )(x, proj, proj, proj, omla, o_f, o_b, memkv, memkv, onorm, gmat, wm, wd, wmem, wout)


def _ffn_kernel(x_ref, nw_ref, wgu_ref, wd_ref, fw_ref, o_ref, *, nf, final_norm):
    x = x_ref[...]
    h = _rms(x, nw_ref[...]).astype(BF16)
    tf = D_FF // nf
    y = x
    for c in range(nf):
        gate = _dot(h, wgu_ref[:, c * tf:(c + 1) * tf])
        up = _dot(h, wgu_ref[:, D_FF + c * tf:D_FF + (c + 1) * tf])
        act = (gate * _sigmoid(gate) * up).astype(BF16)
        y = y + _dot(act, wd_ref[c * tf:(c + 1) * tf, :])
    if final_norm:
        y = _rms(y, fw_ref[...])
    o_ref[...] = y


def _ffn(x, nw, wgu, wd, fw, *, tm, nf, final_norm):
    t = x.shape[0]
    resident = lambda shape: pl.BlockSpec(shape, lambda i: (0, 0), pipeline_mode=pl.Buffered(1))
    return pl.pallas_call(
        functools.partial(_ffn_kernel, nf=nf, final_norm=final_norm),
        grid=(t // tm,),
        in_specs=[pl.BlockSpec((tm, D_MODEL), lambda i: (i, 0)),
                  pl.BlockSpec((1, D_MODEL), lambda i: (0, 0)),
                  resident((D_MODEL, 2 * D_FF)),
                  resident((D_FF, D_MODEL)),
                  pl.BlockSpec((1, D_MODEL), lambda i: (0, 0))],
        out_specs=pl.BlockSpec((tm, D_MODEL), lambda i: (i, 0)),
        out_shape=jax.ShapeDtypeStruct((t, D_MODEL), F32),
        compiler_params=_cparams(("parallel",)),
        name="ffn",
    )(x, nw, wgu, wd, fw)


def _pick(n, pref):
    return pref if n % pref == 0 else n


def _rope_tables(positions):
    b, s = positions.shape
    inv_freq = 1.0 / (ROPE_THETA ** (jnp.arange(0, MLA_ROPE, 2, dtype=F32) / MLA_ROPE))
    ang = positions.astype(F32)[..., None] * inv_freq
    cos = jnp.cos(ang).reshape(b * s, MLA_ROPE // 2)
    sin = jnp.sin(ang).reshape(b * s, MLA_ROPE // 2)
    z64 = jnp.zeros((b * s, MLA_NOPE), F32)
    z32 = jnp.zeros((b * s, 32), F32)
    ck = jnp.concatenate([z64, cos, cos, z32], axis=1)
    sk = jnp.concatenate([z64, sin, sin, z32], axis=1)
    qscale = (MLA_QK ** -0.5) * LOG2E
    cq = jnp.concatenate([jnp.ones_like(z64), cos, cos, z32], axis=1) * qscale
    sq = sk * qscale
    to_t = lambda a: a.reshape(b, s, 128).transpose(0, 2, 1)
    return ck, sk, to_t(cq), to_t(sq)


def _layer_weights(w_in, w_uq, w_ukv):
    d = w_in.shape[0]
    w_in = w_in.astype(BF16)
    z = lambda n: jnp.zeros((d, n), BF16)
    kr = w_in[:, 512:544]
    w_re = jnp.concatenate([
        w_in[:, 3136:6208], w_in[:, 544:2080], w_in[:, 2080:2592], w_in[:, 2624:3136],
        w_in[:, 0:256], w_in[:, 256:512],
        w_in[:, 2592:2624], z(32), kr, z(32),
        z(64), -kr[:, 16:32], kr[:, 0:16], z(32)], axis=1)

    uq = w_uq.reshape(MLA_Q_LORA, MLA_HEADS, MLA_QK)
    zq = lambda n: jnp.zeros((MLA_Q_LORA, MLA_HEADS, n), w_uq.dtype)
    x1, x2 = uq[..., 64:80], uq[..., 80:96]
    wq1 = jnp.concatenate([uq[..., :64], x1, x2, zq(32)], axis=-1).reshape(MLA_Q_LORA, 1024)
    wq2 = jnp.concatenate([zq(64), -x2, x1, zq(32)], axis=-1).reshape(MLA_Q_LORA, 1024)
    ukv = w_ukv.reshape(MLA_KV_LORA, MLA_HEADS, MLA_NOPE + MLA_V)
    wk = jnp.concatenate([ukv[..., :64], jnp.zeros((MLA_KV_LORA, MLA_HEADS, 64), w_ukv.dtype)],
                         axis=-1).reshape(MLA_KV_LORA, 1024)
    wv = ukv[..., 64:].reshape(MLA_KV_LORA, MLA_HEADS * MLA_V)
    return w_re, wq1.T.astype(BF16), wq2.T.astype(BF16), wk.astype(BF16), wv.T.astype(BF16)


def kernel(x, mem, positions, norm_mix, w_in, mla_q_norm, mla_w_uq, mla_kv_norm, mla_w_ukv, dn_conv, dn_a_log, dn_dt_bias, dn_out_norm, mem_norm, mem_w_kv, w_branch_mla, w_branch_dn, w_branch_mem, w_out, norm_ffn, ffn_w_gate_up, ffn_w_down, final_norm):
    b, s, d = x.shape
    t = b * s
    depth = w_in.shape[0]
    mt = mem.shape[1]
    ck, sk, cqt, sqt = _rope_tables(positions)
    gi = jnp.arange(DN_W) // DN_DK
    gmat = (gi[:, None] == gi[None, :]).astype(BF16)
    xf = x.reshape(t, d)
    memf = mem.reshape(b * mt, d)
    tm = _pick(s, 512)
    for l in range(depth):
        w_re, wq1t, wq2t, wk, wvt = _layer_weights(w_in[l], mla_w_uq[l], mla_w_ukv[l])
        proj, small = _norm_matmul(xf, norm_mix[l][None], w_re, tm=_pick(t, 1024), tn=1280, tail=IN_COLS_P - COL_SMALL)

        qt, kk, vt = _mla_prep(proj, small, ck, sk, cqt, sqt, mla_q_norm[l][None], mla_kv_norm[l][None],
                               wq1t, wq2t, wk, wvt, b=b, s=s, tm=tm)
        o_mla = _flash(qt, kk, vt, b=b, s=s, tq=_pick(s, 512), tk=_pick(s, 256), unroll=8)

        conv_w = jnp.concatenate([dn_conv[l], jnp.zeros((8 - DN_CONV, 3 * DN_W), F32)], axis=0)
        pad16 = lambda v: jnp.concatenate([v.reshape(-1), jnp.zeros((128 - 2 * DN_HEADS,), F32)])
        par = jnp.concatenate([pad16(dn_a_log[l])[None], pad16(dn_dt_bias[l])[None],
                               jnp.zeros((6, 128), F32)], axis=0)
        qn, kn, vv, gb = _dn_prep(proj, small, conv_w, par, gmat, b=b, s=s, tm=tm)
        o_f, o_b = _dn_chunk(qn, kn, vv, gb, b=b, s=s, nch=4)
        n_chunks = s // DN_CHUNK
        perm = lambda a: a.reshape(b, n_chunks, DN_CHUNK, DN_W).swapaxes(1, 2).reshape(t, DN_W)
        o_f, o_b = perm(o_f), perm(o_b)

        memkv = _norm_matmul(memf, mem_norm[l][None], mem_w_kv[l].astype(BF16),
                             tm=b * mt, tn=512)[0].reshape(b, mt, 2 * MEM_W)
        onorm = jnp.tile(dn_out_norm[l], DN_HEADS)[None]
        xf = _merge(xf, proj, o_mla, o_f, o_b, memkv, onorm, gmat,
                    w_branch_mla[l].astype(BF16), w_branch_dn[l].astype(BF16),
                    w_branch_mem[l].astype(BF16), w_out[l].astype(BF16), b=b, s=s, tm=tm)
        xf = _ffn(xf, norm_ffn[l][None], ffn_w_gate_up[l].astype(BF16), ffn_w_down[l].astype(BF16),
                  final_norm[None], tm=tm, nf=2, final_norm=(l == depth - 1))
    return xf.reshape(b, s, d)
```

```python
import functools

import jax
import jax.numpy as jnp
from jax import lax
from jax.experimental import pallas as pl
from jax.experimental.pallas import tpu as pltpu

F32 = jnp.float32
BF16 = jnp.bfloat16

D_MODEL = 1024
EPS = 1e-6
MLA_HEADS = 8
MLA_Q_LORA = 256
MLA_KV_LORA = 256
MLA_NOPE = 64
MLA_ROPE = 32
MLA_V = 64
MLA_QK = MLA_NOPE + MLA_ROPE
ROPE_THETA = 10000.0
DN_HEADS = 8
DN_DK = 64
DN_W = DN_HEADS * DN_DK
DN_CONV = 5
DN_CHUNK = 64
MEM_HEADS = 4
MEM_HD = 128
MEM_W = MEM_HEADS * MEM_HD
D_FF = 2816

COL_GATES = 0
COL_DQKV = 3072
COL_DZ = 4608
COL_MQ = 5120
COL_CQ = 5632
COL_CKV = 5888
COL_SMALL = 6144
COL_SMALL2 = 6272
IN_COLS_P = 6400

HEAD_SLAB = 128
VT_ROWS = 80
LOG2E = 1.4426950408889634
VMEM_LIMIT = 48 * 1024 * 1024


def _cparams(sem):
    return pltpu.CompilerParams(dimension_semantics=sem, vmem_limit_bytes=VMEM_LIMIT)


def _sigmoid(x):
    return 0.5 * jnp.tanh(0.5 * x) + 0.5


def _dot(a, b):
    return jnp.dot(a, b, preferred_element_type=F32)


def _dot_nt(a, b):
    return lax.dot_general(a, b, (((1,), (1,)), ((), ())), preferred_element_type=F32)


def _dot_tn(a, b):
    return lax.dot_general(a, b, (((0,), (0,)), ((), ())), preferred_element_type=F32)


def _split3(x):
    x1 = x.astype(BF16)
    r1 = x - x1.astype(F32)
    x2 = r1.astype(BF16)
    x3 = (r1 - x2.astype(F32)).astype(BF16)
    return x1, x2, x3


def _rms(x, w):
    return x * lax.rsqrt(jnp.mean(x * x, axis=-1, keepdims=True) + EPS) * w


def _norm_matmul_kernel(x_ref, g_ref, w_ref, o_ref, *rest, tail):
    h_ref = rest[-1]

    @pl.when(pl.program_id(1) == 0)
    def _():
        h_ref[...] = _rms(x_ref[...], g_ref[...]).astype(BF16)

    acc = _dot(h_ref[...], w_ref[...])
    o_ref[...] = acc.astype(o_ref.dtype)
    if tail:
        @pl.when(pl.program_id(1) == pl.num_programs(1) - 1)
        def _():
            rest[0][...] = acc[:, acc.shape[1] - tail:]


def _norm_matmul(x, g, w, *, tm, tn, tail=0):
    t, d = x.shape
    n = w.shape[1]
    out_specs = [pl.BlockSpec((tm, tn), lambda i, j: (i, j))]
    out_shape = [jax.ShapeDtypeStruct((t, n), BF16)]
    if tail:
        out_specs.append(pl.BlockSpec((tm, tail), lambda i, j: (i, 0)))
        out_shape.append(jax.ShapeDtypeStruct((t, tail), F32))
    return pl.pallas_call(
        functools.partial(_norm_matmul_kernel, tail=tail),
        grid=(t // tm, n // tn),
        in_specs=[pl.BlockSpec((tm, d), lambda i, j: (i, 0)),
                  pl.BlockSpec((1, d), lambda i, j: (0, 0)),
                  pl.BlockSpec((d, tn), lambda i, j: (0, j))],
        out_specs=out_specs,
        out_shape=out_shape,
        scratch_shapes=[pltpu.VMEM((tm, d), BF16)],
        compiler_params=_cparams(("parallel", "arbitrary")),
        name="norm_matmul",
    )(x, g, w)


def _mla_prep_kernel(cq_ref, ckv_ref, sm_ref, ck_ref, sk_ref, cqt_ref, sqt_ref, qn_ref, kvn_ref,
                     wq1t_ref, wq2t_ref, wk_ref, wvt_ref, qt_out, k_out, vt_out):
    qn = _rms(cq_ref[...].astype(F32), qn_ref[...]).astype(BF16)
    q1t = _dot_nt(wq1t_ref[...], qn)
    q2t = _dot_nt(wq2t_ref[...], qn)
    ct = cqt_ref[0]
    st = sqt_ref[0]
    for h in range(MLA_HEADS):
        hs = slice(h * HEAD_SLAB, (h + 1) * HEAD_SLAB)
        qt_out[0, hs, :] = (q1t[hs] * ct + q2t[hs] * st).astype(BF16)

    kvn = _rms(ckv_ref[...].astype(F32), kvn_ref[...]).astype(BF16)
    kk = _dot(kvn, wk_ref[...])
    sm = sm_ref[...]
    kr = sm[:, :128] * ck_ref[...] + sm[:, 128:] * sk_ref[...]
    for h in range(MLA_HEADS):
        hs = slice(h * HEAD_SLAB, (h + 1) * HEAD_SLAB)
        k_out[:, hs] = (kk[:, hs] + kr).astype(BF16)

    vt = _dot_nt(wvt_ref[...], kvn)
    tm = vt.shape[1]
    ones = jnp.ones((VT_ROWS - MLA_V, tm), BF16)
    for h in range(MLA_HEADS):
        vt_out[0, h * VT_ROWS:h * VT_ROWS + MLA_V, :] = vt[h * MLA_V:(h + 1) * MLA_V].astype(BF16)
        vt_out[0, h * VT_ROWS + MLA_V:(h + 1) * VT_ROWS, :] = ones


def _mla_prep(proj, small, ck, sk, cqt, sqt, qnorm, kvnorm, wq1t, wq2t, wk, wvt, *, b, s, tm):
    nb = s // tm
    t = b * s
    row = lambda bi, i: bi * nb + i
    const = lambda shape: pl.BlockSpec(shape, lambda bi, i: (0,) * len(shape))
    return pl.pallas_call(
        _mla_prep_kernel,
        grid=(b, nb),
        in_specs=[pl.BlockSpec((tm, 256), lambda bi, i: (row(bi, i), COL_CQ // 256)),
                  pl.BlockSpec((tm, 256), lambda bi, i: (row(bi, i), COL_CKV // 256)),
                  pl.BlockSpec((tm, 256), lambda bi, i: (row(bi, i), 0)),
                  pl.BlockSpec((tm, 128), lambda bi, i: (row(bi, i), 0)),
                  pl.BlockSpec((tm, 128), lambda bi, i: (row(bi, i), 0)),
                  pl.BlockSpec((1, 128, tm), lambda bi, i: (bi, 0, i)),
                  pl.BlockSpec((1, 128, tm), lambda bi, i: (bi, 0, i)),
                  const((1, 256)), const((1, 256)),
                  const((1024, 256)), const((1024, 256)), const((256, 1024)), const((512, 256))],
        out_specs=[pl.BlockSpec((1, 1024, tm), lambda bi, i: (bi, 0, i)),
                   pl.BlockSpec((tm, 1024), lambda bi, i: (row(bi, i), 0)),
                   pl.BlockSpec((1, MLA_HEADS * VT_ROWS, tm), lambda bi, i: (bi, 0, i))],
        out_shape=[jax.ShapeDtypeStruct((b, 1024, s), BF16),
                   jax.ShapeDtypeStruct((t, 1024), BF16),
                   jax.ShapeDtypeStruct((b, MLA_HEADS * VT_ROWS, s), BF16)],
        compiler_params=_cparams(("parallel", "parallel")),
        name="mla_prep",
    )(proj, proj, small, ck, sk, cqt, sqt, qnorm, kvnorm, wq1t, wq2t, wk, wvt)


def _chunk_start(j, tk):
    return j * tk if isinstance(j, int) else pl.multiple_of(j * tk, tk)


def _flash_kernel(qt_ref, k_ref, vt_ref, o_ref, m_ref, acc_ref, s_buf, p_buf, a_buf, *, tq, tk, nq, nk, unroll):
    total = nq * nk
    lognk = nk.bit_length() - 1
    m_ref[...] = jnp.full(m_ref.shape, -jnp.inf, F32)
    acc_ref[...] = jnp.zeros(acc_ref.shape, F32)

    def split(n):
        if isinstance(n, int):
            return n // nk, n % nk
        return n >> lognk, n & (nk - 1)

    def scores(n, par):
        qi, j = split(n)
        q0 = _chunk_start(qi, tq)
        r0 = _chunk_start(j, tk)
        for h in range(2):
            hs = slice(h * HEAD_SLAB, (h + 1) * HEAD_SLAB)
            s_buf[par, h] = _dot(k_ref[pl.ds(r0, tk), hs],
                                 qt_ref[0, hs, pl.ds(q0, tq)]).astype(s_buf.dtype)

    def softmax(n, par):
        _, j = split(n)
        for h in range(2):
            st = s_buf[par, h]
            m_old = jnp.where(j == 0, -jnp.inf, m_ref[h])
            m_new = jnp.maximum(m_old, jnp.max(st, axis=0, keepdims=True).astype(F32))
            a_buf[par, h] = jnp.exp2(m_old - m_new)
            p_buf[par, h] = jnp.exp2(st - m_new.astype(st.dtype)).astype(BF16)
            m_ref[h] = m_new

    def pv(n, par):
        _, j = split(n)
        r0 = _chunk_start(j, tk)
        for h in range(2):
            vc = vt_ref[0, h * VT_ROWS:(h + 1) * VT_ROWS, pl.ds(r0, tk)]
            acc_ref[h] = a_buf[par, h] * acc_ref[h] + _dot(vc, p_buf[par, h])

    def finalize(qi):
        outs = []
        for h in range(2):
            a = acc_ref[h]
            outs.append(a[0:MLA_V] / a[MLA_V:MLA_V + 1])
        ot = jnp.concatenate(outs, axis=0)
        o_ref[pl.ds(_chunk_start(qi, tq), tq), :] = ot.T.astype(o_ref.dtype)

    scores(0, 0)
    scores(1, 1)
    softmax(0, 0)
    trips = (total - 2) // unroll
    assert trips == 0 or (nk % unroll == 0 and nk == 1 << lognk and unroll % 2 == 0)

    def body(i, carry):
        for u in range(unroll):
            n = unroll * i + u
            pv(n, u % 2)
            softmax(n + 1, (u + 1) % 2)
            scores(n + 2, u % 2)
        n_last = unroll * i + unroll - 1

        @pl.when((n_last & (nk - 1)) == nk - 1)
        def _():
            finalize(n_last >> lognk)
        return carry

    lax.fori_loop(0, trips, body, 0)
    for n in range(trips * unroll, total):
        pv(n, n % 2)
        if n + 1 < total:
            softmax(n + 1, (n + 1) % 2)
        if n + 2 < total:
            scores(n + 2, n % 2)
        if n % nk == nk - 1:
            finalize(n // nk)


def _flash(qt, k, vt, *, b, s, tq, tk, unroll=4, s_dtype=BF16):
    nq = s // tq
    nk = s // tk
    hp = MLA_HEADS // 2
    return pl.pallas_call(
        functools.partial(_flash_kernel, tq=tq, tk=tk, nq=nq, nk=nk, unroll=unroll),
        grid=(b, hp),
        in_specs=[pl.BlockSpec((1, 2 * HEAD_SLAB, s), lambda bi, p: (bi, p, 0)),
                  pl.BlockSpec((s, 2 * HEAD_SLAB), lambda bi, p: (bi, p)),
                  pl.BlockSpec((1, 2 * VT_ROWS, s), lambda bi, p: (bi, p, 0))],
        out_specs=pl.BlockSpec((s, 2 * MLA_V), lambda bi, p: (bi, p)),
        out_shape=jax.ShapeDtypeStruct((b * s, MLA_HEADS * MLA_V), BF16),
        scratch_shapes=[pltpu.VMEM((2, 1, tq), F32), pltpu.VMEM((2, VT_ROWS, tq), F32),
                        pltpu.VMEM((2, 2, tk, tq), s_dtype), pltpu.VMEM((2, 2, tk, tq), BF16),
                        pltpu.VMEM((2, 2, 1, tq), F32)],
        compiler_params=_cparams(("parallel", "arbitrary")),
        name="mla_flash",
    )(qt, k, vt)


def _group_sum(z, gmat):
    z1 = z.astype(BF16)
    z2 = (z - z1.astype(F32)).astype(BF16)
    return _dot(z1, gmat) + _dot(z2, gmat)


HALO = 16


def _dn_prep_kernel(x_ref, xp_ref, xn_ref, sm_ref, cw_ref, par_ref, gm_ref,
                    q_out, k_out, v_out, gb_out, pad_ref, *, nb, tm):
    i = pl.program_id(0)
    first = (i % nb) == 0
    last = (i % nb) == nb - 1
    pad_ref[0:HALO, :] = jnp.where(first, 0.0, xp_ref[...].astype(F32))
    pad_ref[HALO:HALO + tm, :] = x_ref[...].astype(F32)
    pad_ref[HALO + tm:2 * HALO + tm, :] = jnp.where(last, 0.0, xn_ref[...].astype(F32))
    cw = cw_ref[...]
    off = HALO - DN_CONV // 2
    y = cw[0:1] * pad_ref[pl.ds(off, tm), :]
    for j in range(1, DN_CONV):
        y = y + cw[j:j + 1] * pad_ref[pl.ds(off + j, tm), :]
    y = y * _sigmoid(y)
    gm = gm_ref[...]
    q = y[:, 0:DN_W]
    k = y[:, DN_W:2 * DN_W]
    q_out[...] = q * lax.rsqrt(_group_sum(q * q, gm) + EPS) * (DN_DK ** -0.5)
    k_out[...] = k * lax.rsqrt(_group_sum(k * k, gm) + EPS)
    v_out[...] = y[:, 2 * DN_W:]

    sm = sm_ref[...]
    par = par_ref[...]
    z = sm + par[1:2]
    softplus = jnp.maximum(z, 0.0) + jnp.log1p(jnp.exp(-jnp.abs(z)))
    g = -jnp.exp(par[0:1]) * softplus
    lane = lax.broadcasted_iota(jnp.int32, sm.shape, 1)
    gb_out[...] = jnp.where(lane < 16, g, _sigmoid(sm))


def _dn_prep(proj, small, conv_w, par, gmat, *, b, s, tm):
    t = b * s
    nb = s // tm
    w3 = 3 * DN_W
    cb = COL_DQKV // w3
    hb = tm // HALO
    last_hb = t // HALO - 1
    return pl.pallas_call(
        functools.partial(_dn_prep_kernel, nb=nb, tm=tm),
        grid=(t // tm,),
        in_specs=[pl.BlockSpec((tm, w3), lambda i: (i, cb)),
                  pl.BlockSpec((HALO, w3), lambda i: (jnp.maximum(i * hb - 1, 0), cb)),
                  pl.BlockSpec((HALO, w3), lambda i: (jnp.minimum((i + 1) * hb, last_hb), cb)),
                  pl.BlockSpec((tm, 128), lambda i: (i, 0)),
                  pl.BlockSpec((8, w3), lambda i: (0, 0)),
                  pl.BlockSpec((8, 128), lambda i: (0, 0)),
                  pl.BlockSpec((DN_W, DN_W), lambda i: (0, 0))],
        out_specs=[pl.BlockSpec((tm, DN_W), lambda i: (i, 0)),
                   pl.BlockSpec((tm, DN_W), lambda i: (i, 0)),
                   pl.BlockSpec((tm, DN_W), lambda i: (i, 0)),
                   pl.BlockSpec((tm, 128), lambda i: (i, 0))],
        out_shape=[jax.ShapeDtypeStruct((t, DN_W), F32)] * 3 + [jax.ShapeDtypeStruct((t, 128), F32)],
        scratch_shapes=[pltpu.VMEM((tm + 2 * HALO, w3), F32)],
        compiler_params=_cparams(("parallel",)),
        name="dn_prep",
    )(proj, proj, proj, small, conv_w, par, gmat)


GRP = 256
NGRP = DN_W // GRP


def _bd_rows(x_bf, bmask):
    return jnp.where(bmask, jnp.concatenate([x_bf] * (GRP // DN_CHUNK), axis=0), jnp.zeros((), BF16))


def _dn_chunk_kernel(qf_ref, kf_ref, vf_ref, gf_ref, qb_ref, kb_ref, vb_ref, gb_ref,
                     of_ref, ob_ref, s_ref, *, nch):
    c = DN_CHUNK

    @pl.when(pl.program_id(1) == 0)
    def _():
        s_ref[...] = jnp.zeros(s_ref.shape, F32)

    ri = lax.broadcasted_iota(jnp.int32, (c, GRP), 0)
    cj = lax.broadcasted_iota(jnp.int32, (c, GRP), 1) & (c - 1)
    br = lax.broadcasted_iota(jnp.int32, (GRP, GRP), 0) >> 6
    bc = lax.broadcasted_iota(jnp.int32, (GRP, GRP), 1) >> 6
    bmask = br == bc
    ident = (ri == cj).astype(F32)
    incl = (ri >= cj, ri <= cj)
    strict = (ri > cj, ri < cj)

    def pmm(a, bmat):
        return _dot(a.astype(BF16), _bd_rows(bmat.astype(BF16), bmask))

    def pmm_nt(a, bmat):
        return _dot_nt(a.astype(BF16), _bd_rows(bmat.astype(BF16), bmask))

    er = lax.broadcasted_iota(jnp.int32, (128, 2 * DN_W), 0)
    ec = lax.broadcasted_iota(jnp.int32, (128, 2 * DN_W), 1) >> 6
    tr = lax.broadcasted_iota(jnp.int32, (c, c), 0)
    tc = lax.broadcasted_iota(jnp.int32, (c, c), 1)
    mr = lax.broadcasted_iota(jnp.int32, (c, DN_W), 0)
    mj = lax.broadcasted_iota(jnp.int32, (c, DN_W), 1) & (c - 1)
    ones_cc = jnp.ones((c, c), BF16)
    tri = ((tc <= tr).astype(BF16), (tc >= tr).astype(BF16))
    mask_t = (mr <= mj, mr >= mj)
    refs = ((qf_ref, kf_ref, vf_ref, gf_ref, of_ref), (qb_ref, kb_ref, vb_ref, gb_ref, ob_ref))

    def rows(d, step):
        i = step if d == 0 else nch - 1 - step
        return slice(i * c, (i + 1) * c)

    def sum3(lhs, rhs3):
        return _dot(lhs, rhs3[0]) + _dot(lhs, rhs3[1]) + _dot(lhs, rhs3[2])

    ds = [(d, step) for step in range(nch) for d in (0, 1)]
    e_gb = [(er == jnp.where(ec < DN_HEADS, d * DN_HEADS + ec, 8 + d * DN_HEADS + ec)).astype(BF16)
            for d in (0, 1)]
    g3 = {u: jnp.concatenate(_split3(refs[u[0]][3][rows(*u), :]), axis=0) for u in ds}
    gbx = {u: _dot(g3[u], e_gb[u[0]]) for u in ds}
    gbx = {u: gbx[u][0:c] + gbx[u][c:2 * c] + gbx[u][2 * c:3 * c] for u in ds}
    gexp = {u: gbx[u][:, :DN_W] for u in ds}
    bexp = {u: gbx[u][:, DN_W:] for u in ds}
    ge3 = {u: _split3(gexp[u]) for u in ds}
    gm3 = {u: _split3(jnp.where(mask_t[u[0]], gexp[u], 0.0)) for u in ds}
    gcrow = {u: sum3(tri[u[0]], ge3[u]) for u in ds}
    gccol = {u: sum3(ones_cc, gm3[u]) for u in ds}

    units = [(d, step, g) for step in range(nch) for d in (0, 1) for g in range(NGRP)]
    sl = lambda g: slice(g * GRP, (g + 1) * GRP)
    kg = {u: refs[u[0]][1][rows(u[0], u[1]), sl(u[2])] for u in units}
    qg = {u: refs[u[0]][0][rows(u[0], u[1]), sl(u[2])] for u in units}
    vg = {u: refs[u[0]][2][rows(u[0], u[1]), sl(u[2])] for u in units}
    bg = {u: bexp[u[:2]][:, sl(u[2])] for u in units}
    gc = {u: gcrow[u[:2]][:, sl(u[2])] for u in units}
    dec = {u: jnp.exp(jnp.where(incl[u[0]], gc[u] - gccol[u[:2]][:, sl(u[2])], -jnp.inf)) for u in units}
    egc = {u: jnp.exp(gc[u]) for u in units}
    kq = {u: pmm_nt(jnp.concatenate([kg[u], qg[u]], axis=0), kg[u]) for u in units}
    kk = {u: kq[u][0:c] for u in units}
    qk = {u: kq[u][c:] for u in units}
    lm = {u: jnp.where(strict[u[0]], kk[u] * bg[u] * dec[u], 0.0) for u in units}

    dd = {u: jnp.where((ri >> 3) == (cj >> 3), lm[u], 0.0) for u in units}
    d2 = {u: pmm(dd[u], dd[u]) for u in units}
    d4 = {u: pmm(d2[u], d2[u]) for u in units}
    x = {u: pmm(ident - dd[u], ident + d2[u]) for u in units}
    x = {u: pmm(x[u], ident + d4[u]) for u in units}
    for sh in (3, 4, 5):
        off = ((ri >> (sh + 1)) == (cj >> (sh + 1))) & ((ri >> sh) != (cj >> sh))
        xm = {u: pmm(x[u], jnp.where(off, lm[u], 0.0)) for u in units}
        x = {u: x[u] - pmm(xm[u], x[u]) for u in units}
    uu = {u: pmm(x[u], vg[u] * bg[u]) for u in units}
    ww = {u: pmm(x[u], kg[u] * bg[u] * egc[u]).astype(BF16) for u in units}
    att = {u: jnp.where(incl[u[0]], qk[u] * dec[u], 0.0).astype(BF16) for u in units}
    wq = {u: jnp.concatenate([ww[u], (qg[u] * egc[u]).astype(BF16)], axis=0) for u in units}
    gl = {u: (gc[u][c - 1:c] if u[0] == 0 else gc[u][0:1]) for u in units}
    kdec = {u: (kg[u] * jnp.exp(gl[u] - gc[u])).astype(BF16) for u in units}
    egl = {u: jnp.exp(gl[u]) for u in units}

    for step in range(nch):
        cur = [(d, step, g) for d in (0, 1) for g in range(NGRP)]
        st = {u: s_ref[u[0] * NGRP + u[2]] for u in cur}
        ws = {u: _dot(wq[u], st[u].astype(BF16)) for u in cur}
        vnew = {u: uu[u] - ws[u][0:c] for u in cur}
        vbd = {u: _bd_rows(vnew[u].astype(BF16), bmask) for u in cur}
        for u in cur:
            refs[u[0]][4][rows(u[0], step), sl(u[2])] = ws[u][c:] + _dot(att[u], vbd[u])
        upd = {u: _dot_tn(kdec[u], vnew[u].astype(BF16)) for u in cur}
        for u in cur:
            s_ref[u[0] * NGRP + u[2]] = st[u] * egl[u] + jnp.where(bmask, upd[u], 0.0)


def _dn_chunk(qn, kn, vv, gb, *, b, s, nch):
    n = s // (DN_CHUNK * nch)
    t = b * s
    fwd = lambda bi, ci: (bi * n + ci, 0)
    bwd = lambda bi, ci: (bi * n + n - 1 - ci, 0)
    wide = lambda im: pl.BlockSpec((DN_CHUNK * nch, DN_W), im)
    small = lambda im: pl.BlockSpec((DN_CHUNK * nch, 128), im)
    return pl.pallas_call(
        functools.partial(_dn_chunk_kernel, nch=nch),
        grid=(b, n),
        in_specs=[wide(fwd), wide(fwd), wide(fwd), small(fwd), wide(bwd), wide(bwd), wide(bwd), small(bwd)],
        out_specs=[wide(fwd), wide(bwd)],
        out_shape=[jax.ShapeDtypeStruct((t, DN_W), F32)] * 2,
        scratch_shapes=[pltpu.VMEM((2 * NGRP, GRP, GRP), F32)],
        compiler_params=_cparams(("parallel", "arbitrary")),
        name="dn_chunk",
    )(qn, kn, vv, gb, qn, kn, vv, gb)


def _merge_kernel(x_ref, gates_ref, dz_ref, mq_ref, omla_ref, of_ref, ob_ref, mk_ref, mv_ref,
                  onorm_ref, gm_ref, wm_ref, wd_ref, wmem_ref, wout_ref, o_ref):
    o = of_ref[...] + ob_ref[...]
    ms = _group_sum(o * o, gm_ref[...]) * (1.0 / DN_DK)
    dz = dz_ref[...].astype(F32)
    o_dn = o * lax.rsqrt(ms + EPS) * onorm_ref[...] * (dz * _sigmoid(dz))

    mq = mq_ref[...]
    mk = mk_ref[0]
    mv = mv_ref[0]
    outs = []
    for h in range(MEM_HEADS):
        hs = slice(h * MEM_HD, (h + 1) * MEM_HD)
        sc = _dot_nt(mq[:, hs], mk[:, hs]) * (MEM_HD ** -0.5)
        p = jnp.exp(sc - jnp.max(sc, axis=-1, keepdims=True))
        p = p / jnp.sum(p, axis=-1, keepdims=True)
        outs.append(_dot(p.astype(BF16), mv[:, hs]))
    o_mem = jnp.concatenate(outs, axis=-1)

    gates = gates_ref[...].astype(F32)
    merged = (_sigmoid(gates[:, 0:D_MODEL]) * _dot(omla_ref[...], wm_ref[...])
              + _sigmoid(gates[:, D_MODEL:2 * D_MODEL]) * _dot(o_dn.astype(BF16), wd_ref[...])
              + _sigmoid(gates[:, 2 * D_MODEL:]) * _dot(o_mem.astype(BF16), wmem_ref[...]))
    o_ref[...] = x_ref[...] + _dot(merged.astype(BF16), wout_ref[...])


def _merge(x, proj, omla, o_f, o_b, memkv, onorm, gmat, wm, wd, wmem, wout, *, b, s, tm):
    t = b * s
    nb = s // tm
    mt = memkv.shape[1]
    const = lambda shape: pl.BlockSpec(shape, lambda i: (0,) * len(shape))
    return pl.pallas_call(
        _merge_kernel,
        grid=(t // tm,),
        in_specs=[pl.BlockSpec((tm, D_MODEL), lambda i: (i, 0)),
                  pl.BlockSpec((tm, 3 * D_MODEL), lambda i: (i, COL_GATES // (3 * D_MODEL))),
                  pl.BlockSpec((tm, DN_W), lambda i: (i, COL_DZ // DN_W)),
                  pl.BlockSpec((tm, MEM_W), lambda i: (i, COL_MQ // MEM_W)),
                  pl.BlockSpec((tm, DN_W), lambda i: (i, 0)),
                  pl.BlockSpec((tm, DN_W), lambda i: (i, 0)),
                  pl.BlockSpec((tm, DN_W), lambda i: (i, 0)),
                  pl.BlockSpec((1, mt, MEM_W), lambda i: (i // nb, 0, 0)),
                  pl.BlockSpec((1, mt, MEM_W), lambda i: (i // nb, 0, 1)),
                  const((1, DN_W)), const((DN_W, DN_W)),
                  const((DN_W, D_MODEL)), const((DN_W, D_MODEL)), const((MEM_W, D_MODEL)),
                  const((D_MODEL, D_MODEL))],
        out_specs=pl.BlockSpec((tm, D_MODEL), lambda i: (i, 0)),
        out_shape=jax.ShapeDtypeStruct((t, D_MODEL), F32),
        compiler_params=_cparams(("parallel",)),
        name="merge",
    )(x, proj, proj, proj, omla, o_f, o_b, memkv, memkv, onorm, gmat, wm, wd, wmem, wout)


def _ffn_kernel(x_ref, nw_ref, wgu_ref, wd_ref, fw_ref, o_ref, *, nf, final_norm):
    x = x_ref[...]
    h = _rms(x, nw_ref[...]).astype(BF16)
    tf = D_FF // nf
    y = x
    for c in range(nf):
        gate = _dot(h, wgu_ref[:, c * tf:(c + 1) * tf])
        up = _dot(h, wgu_ref[:, D_FF + c * tf:D_FF + (c + 1) * tf])
        act = (gate * _sigmoid(gate) * up).astype(BF16)
        y = y + _dot(act, wd_ref[c * tf:(c + 1) * tf, :])
    if final_norm:
        y = _rms(y, fw_ref[...])
    o_ref[...] = y


def _ffn(x, nw, wgu, wd, fw, *, tm, nf, final_norm):
    t = x.shape[0]
    resident = lambda shape: pl.BlockSpec(shape, lambda i: (0, 0), pipeline_mode=pl.Buffered(1))
    return pl.pallas_call(
        functools.partial(_ffn_kernel, nf=nf, final_norm=final_norm),
        grid=(t // tm,),
        in_specs=[pl.BlockSpec((tm, D_MODEL), lambda i: (i, 0)),
                  pl.BlockSpec((1, D_MODEL), lambda i: (0, 0)),
                  resident((D_MODEL, 2 * D_FF)),
                  resident((D_FF, D_MODEL)),
                  pl.BlockSpec((1, D_MODEL), lambda i: (0, 0))],
        out_specs=pl.BlockSpec((tm, D_MODEL), lambda i: (i, 0)),
        out_shape=jax.ShapeDtypeStruct((t, D_MODEL), F32),
        compiler_params=_cparams(("parallel",)),
        name="ffn",
    )(x, nw, wgu, wd, fw)


def _pick(n, pref):
    return pref if n % pref == 0 else n


def _rope_tables(positions):
    b, s = positions.shape
    half = MLA_ROPE // 2
    inv_freq = 1.0 / (ROPE_THETA ** (jnp.arange(0, MLA_ROPE, 2, dtype=F32) / MLA_ROPE))
    ang = (positions.astype(F32)[..., None] * inv_freq).reshape(-1, 128)
    cos, sin = lax.optimization_barrier((jnp.cos(ang), jnp.sin(ang)))
    cos = cos.reshape(b * s, half)
    sin = sin.reshape(b * s, half)
    z64 = jnp.zeros((b * s, MLA_NOPE), F32)
    z32 = jnp.zeros((b * s, 32), F32)
    ck = jnp.concatenate([z64, cos, cos, z32], axis=1)
    sk = jnp.concatenate([z64, sin, sin, z32], axis=1)
    qscale = (MLA_QK ** -0.5) * LOG2E
    cq = jnp.concatenate([jnp.ones_like(z64), cos, cos, z32], axis=1) * qscale
    sq = sk * qscale
    to_t = lambda a: a.reshape(b, s, 128).transpose(0, 2, 1)
    return ck, sk, to_t(cq), to_t(sq)


def _layer_weights(w_in, w_uq, w_ukv):
    d = w_in.shape[0]
    w_in = w_in.astype(BF16)
    z = lambda n: jnp.zeros((d, n), BF16)
    kr = w_in[:, 512:544]
    w_re = jnp.concatenate([
        w_in[:, 3136:6208], w_in[:, 544:2080], w_in[:, 2080:2592], w_in[:, 2624:3136],
        w_in[:, 0:256], w_in[:, 256:512],
        w_in[:, 2592:2624], z(32), kr, z(32),
        z(64), -kr[:, 16:32], kr[:, 0:16], z(32)], axis=1)

    uq = w_uq.reshape(MLA_Q_LORA, MLA_HEADS, MLA_QK)
    zq = lambda n: jnp.zeros((MLA_Q_LORA, MLA_HEADS, n), w_uq.dtype)
    x1, x2 = uq[..., 64:80], uq[..., 80:96]
    wq1 = jnp.concatenate([uq[..., :64], x1, x2, zq(32)], axis=-1).reshape(MLA_Q_LORA, 1024)
    wq2 = jnp.concatenate([zq(64), -x2, x1, zq(32)], axis=-1).reshape(MLA_Q_LORA, 1024)
    ukv = w_ukv.reshape(MLA_KV_LORA, MLA_HEADS, MLA_NOPE + MLA_V)
    wk = jnp.concatenate([ukv[..., :64], jnp.zeros((MLA_KV_LORA, MLA_HEADS, 64), w_ukv.dtype)],
                         axis=-1).reshape(MLA_KV_LORA, 1024)
    wv = ukv[..., 64:].reshape(MLA_KV_LORA, MLA_HEADS * MLA_V)
    return w_re, wq1.T.astype(BF16), wq2.T.astype(BF16), wk.astype(BF16), wv.T.astype(BF16)


def kernel(x, mem, positions, norm_mix, w_in, mla_q_norm, mla_w_uq, mla_kv_norm, mla_w_ukv, dn_conv, dn_a_log, dn_dt_bias, dn_out_norm, mem_norm, mem_w_kv, w_branch_mla, w_branch_dn, w_branch_mem, w_out, norm_ffn, ffn_w_gate_up, ffn_w_down, final_norm):
    b, s, d = x.shape
    t = b * s
    depth = w_in.shape[0]
    mt = mem.shape[1]
    ck, sk, cqt, sqt = _rope_tables(positions)
    gi = jnp.arange(DN_W) // DN_DK
    gmat = (gi[:, None] == gi[None, :]).astype(BF16)
    xf = x.reshape(t, d)
    memf = mem.reshape(b * mt, d)
    tm = _pick(s, 512)
    for l in range(depth):
        w_re, wq1t, wq2t, wk, wvt = _layer_weights(w_in[l], mla_w_uq[l], mla_w_ukv[l])
        proj, small = _norm_matmul(xf, norm_mix[l][None], w_re, tm=_pick(t, 1024), tn=1280, tail=IN_COLS_P - COL_SMALL)

        qt, kk, vt = _mla_prep(proj, small, ck, sk, cqt, sqt, mla_q_norm[l][None], mla_kv_norm[l][None],
                               wq1t, wq2t, wk, wvt, b=b, s=s, tm=tm)
        o_mla = _flash(qt, kk, vt, b=b, s=s, tq=_pick(s, 512), tk=_pick(s, 256), unroll=8)

        conv_w = jnp.concatenate([dn_conv[l], jnp.zeros((8 - DN_CONV, 3 * DN_W), F32)], axis=0)
        pad16 = lambda v: jnp.concatenate([v.reshape(-1), jnp.zeros((128 - 2 * DN_HEADS,), F32)])
        par = jnp.concatenate([pad16(dn_a_log[l])[None], pad16(dn_dt_bias[l])[None],
                               jnp.zeros((6, 128), F32)], axis=0)
        qn, kn, vv, gb = _dn_prep(proj, small, conv_w, par, gmat, b=b, s=s, tm=tm)
        o_f, o_b = _dn_chunk(qn, kn, vv, gb, b=b, s=s, nch=4)
        n_chunks = s // DN_CHUNK
        perm = lambda a: a.reshape(b, n_chunks, DN_CHUNK, DN_W).swapaxes(1, 2).reshape(t, DN_W)
        o_f, o_b = perm(o_f), perm(o_b)

        memkv = _norm_matmul(memf, mem_norm[l][None], mem_w_kv[l].astype(BF16),
                             tm=b * mt, tn=512)[0].reshape(b, mt, 2 * MEM_W)
        onorm = jnp.tile(dn_out_norm[l], DN_HEADS)[None]
        xf = _merge(xf, proj, o_mla, o_f, o_b, memkv, onorm, gmat,
                    w_branch_mla[l].astype(BF16), w_branch_dn[l].astype(BF16),
                    w_branch_mem[l].astype(BF16), w_out[l].astype(BF16), b=b, s=s, tm=tm)
        xf = _ffn(xf, norm_ffn[l][None], ffn_w_gate_up[l].astype(BF16), ffn_w_down[l].astype(BF16),
                  final_norm[None], tm=tm, nf=2, final_norm=(l == depth - 1))
    return xf.reshape(b, s, d)
```

```python
import functools

import jax
import jax.numpy as jnp
from jax import lax
from jax.experimental import pallas as pl
from jax.experimental.pallas import tpu as pltpu

F32 = jnp.float32
BF16 = jnp.bfloat16

D_MODEL = 1024
EPS = 1e-6
MLA_HEADS = 8
MLA_Q_LORA = 256
MLA_KV_LORA = 256
MLA_NOPE = 64
MLA_ROPE = 32
MLA_V = 64
MLA_QK = MLA_NOPE + MLA_ROPE
ROPE_THETA = 10000.0
DN_HEADS = 8
DN_DK = 64
DN_W = DN_HEADS * DN_DK
DN_CONV = 5
DN_CHUNK = 64
MEM_HEADS = 4
MEM_HD = 128
MEM_W = MEM_HEADS * MEM_HD
D_FF = 2816

COL_GATES = 0
COL_DQKV = 3072
COL_DZ = 4608
COL_MQ = 5120
COL_CQ = 5632
COL_CKV = 5888
COL_SMALL = 6144
COL_SMALL2 = 6272
IN_COLS_P = 6400

HEAD_SLAB = 128
VT_ROWS = 80
LOG2E = 1.4426950408889634
VMEM_LIMIT = 48 * 1024 * 1024


def _cparams(sem):
    return pltpu.CompilerParams(dimension_semantics=sem, vmem_limit_bytes=VMEM_LIMIT)


def _sigmoid(x):
    return 0.5 * jnp.tanh(0.5 * x) + 0.5


def _dot(a, b):
    return jnp.dot(a, b, preferred_element_type=F32)


def _dot_nt(a, b):
    return lax.dot_general(a, b, (((1,), (1,)), ((), ())), preferred_element_type=F32)


def _dot_tn(a, b):
    return lax.dot_general(a, b, (((0,), (0,)), ((), ())), preferred_element_type=F32)


def _split3(x):
    x1 = x.astype(BF16)
    r1 = x - x1.astype(F32)
    x2 = r1.astype(BF16)
    x3 = (r1 - x2.astype(F32)).astype(BF16)
    return x1, x2, x3


def _rms(x, w):
    return x * lax.rsqrt(jnp.mean(x * x, axis=-1, keepdims=True) + EPS) * w


def _norm_matmul_kernel(x_ref, g_ref, w_ref, o_ref, *rest, tail):
    h_ref = rest[-1]

    @pl.when(pl.program_id(1) == 0)
    def _():
        h_ref[...] = _rms(x_ref[...], g_ref[...]).astype(BF16)

    acc = _dot(h_ref[...], w_ref[...])
    o_ref[...] = acc.astype(o_ref.dtype)
    if tail:
        @pl.when(pl.program_id(1) == pl.num_programs(1) - 1)
        def _():
            rest[0][...] = acc[:, acc.shape[1] - tail:]


def _norm_matmul(x, g, w, *, tm, tn, tail=0):
    t, d = x.shape
    n = w.shape[1]
    out_specs = [pl.BlockSpec((tm, tn), lambda i, j: (i, j))]
    out_shape = [jax.ShapeDtypeStruct((t, n), BF16)]
    if tail:
        out_specs.append(pl.BlockSpec((tm, tail), lambda i, j: (i, 0)))
        out_shape.append(jax.ShapeDtypeStruct((t, tail), F32))
    return pl.pallas_call(
        functools.partial(_norm_matmul_kernel, tail=tail),
        grid=(t // tm, n // tn),
        in_specs=[pl.BlockSpec((tm, d), lambda i, j: (i, 0)),
                  pl.BlockSpec((1, d), lambda i, j: (0, 0)),
                  pl.BlockSpec((d, tn), lambda i, j: (0, j))],
        out_specs=out_specs,
        out_shape=out_shape,
        scratch_shapes=[pltpu.VMEM((tm, d), BF16)],
        compiler_params=_cparams(("parallel", "arbitrary")),
        name="norm_matmul",
    )(x, g, w)


def _mla_prep_kernel(cq_ref, ckv_ref, sm_ref, ck_ref, sk_ref, cqt_ref, sqt_ref, qn_ref, kvn_ref,
                     wq1t_ref, wq2t_ref, wk_ref, wvt_ref, qt_out, k_out, vt_out):
    qn = _rms(cq_ref[...].astype(F32), qn_ref[...]).astype(BF16)
    q1t = _dot_nt(wq1t_ref[...], qn)
    q2t = _dot_nt(wq2t_ref[...], qn)
    ct = cqt_ref[0]
    st = sqt_ref[0]
    for h in range(MLA_HEADS):
        hs = slice(h * HEAD_SLAB, (h + 1) * HEAD_SLAB)
        qt_out[0, hs, :] = (q1t[hs] * ct + q2t[hs] * st).astype(BF16)

    kvn = _rms(ckv_ref[...].astype(F32), kvn_ref[...]).astype(BF16)
    kk = _dot(kvn, wk_ref[...])
    sm = sm_ref[...]
    kr = sm[:, :128] * ck_ref[...] + sm[:, 128:] * sk_ref[...]
    for h in range(MLA_HEADS):
        hs = slice(h * HEAD_SLAB, (h + 1) * HEAD_SLAB)
        k_out[:, hs] = (kk[:, hs] + kr).astype(BF16)

    vt = _dot_nt(wvt_ref[...], kvn)
    tm = vt.shape[1]
    ones = jnp.ones((VT_ROWS - MLA_V, tm), BF16)
    for h in range(MLA_HEADS):
        vt_out[0, h * VT_ROWS:h * VT_ROWS + MLA_V, :] = vt[h * MLA_V:(h + 1) * MLA_V].astype(BF16)
        vt_out[0, h * VT_ROWS + MLA_V:(h + 1) * VT_ROWS, :] = ones


def _mla_prep(proj, small, ck, sk, cqt, sqt, qnorm, kvnorm, wq1t, wq2t, wk, wvt, *, b, s, tm):
    nb = s // tm
    t = b * s
    row = lambda bi, i: bi * nb + i
    const = lambda shape: pl.BlockSpec(shape, lambda bi, i: (0,) * len(shape))
    return pl.pallas_call(
        _mla_prep_kernel,
        grid=(b, nb),
        in_specs=[pl.BlockSpec((tm, 256), lambda bi, i: (row(bi, i), COL_CQ // 256)),
                  pl.BlockSpec((tm, 256), lambda bi, i: (row(bi, i), COL_CKV // 256)),
                  pl.BlockSpec((tm, 256), lambda bi, i: (row(bi, i), 0)),
                  pl.BlockSpec((tm, 128), lambda bi, i: (row(bi, i), 0)),
                  pl.BlockSpec((tm, 128), lambda bi, i: (row(bi, i), 0)),
                  pl.BlockSpec((1, 128, tm), lambda bi, i: (bi, 0, i)),
                  pl.BlockSpec((1, 128, tm), lambda bi, i: (bi, 0, i)),
                  const((1, 256)), const((1, 256)),
                  const((1024, 256)), const((1024, 256)), const((256, 1024)), const((512, 256))],
        out_specs=[pl.BlockSpec((1, 1024, tm), lambda bi, i: (bi, 0, i)),
                   pl.BlockSpec((tm, 1024), lambda bi, i: (row(bi, i), 0)),
                   pl.BlockSpec((1, MLA_HEADS * VT_ROWS, tm), lambda bi, i: (bi, 0, i))],
        out_shape=[jax.ShapeDtypeStruct((b, 1024, s), BF16),
                   jax.ShapeDtypeStruct((t, 1024), BF16),
                   jax.ShapeDtypeStruct((b, MLA_HEADS * VT_ROWS, s), BF16)],
        compiler_params=_cparams(("parallel", "parallel")),
        name="mla_prep",
    )(proj, proj, small, ck, sk, cqt, sqt, qnorm, kvnorm, wq1t, wq2t, wk, wvt)


def _chunk_start(j, tk):
    return j * tk if isinstance(j, int) else pl.multiple_of(j * tk, tk)


def _flash_kernel(qt_ref, k_ref, vt_ref, o_ref, m_ref, acc_ref, s_buf, p_buf, a_buf, *, tq, tk, nq, nk, unroll):
    total = nq * nk
    lognk = nk.bit_length() - 1
    m_ref[...] = jnp.full(m_ref.shape, -jnp.inf, F32)
    acc_ref[...] = jnp.zeros(acc_ref.shape, F32)

    def split(n):
        if isinstance(n, int):
            return n // nk, n % nk
        return n >> lognk, n & (nk - 1)

    def scores(n, par):
        qi, j = split(n)
        q0 = _chunk_start(qi, tq)
        r0 = _chunk_start(j, tk)
        for h in range(2):
            hs = slice(h * HEAD_SLAB, (h + 1) * HEAD_SLAB)
            s_buf[par, h] = _dot(k_ref[pl.ds(r0, tk), hs],
                                 qt_ref[0, hs, pl.ds(q0, tq)]).astype(s_buf.dtype)

    def softmax(n, par):
        _, j = split(n)
        for h in range(2):
            st = s_buf[par, h]
            m_old = jnp.where(j == 0, -jnp.inf, m_ref[h])
            m_new = jnp.maximum(m_old, jnp.max(st, axis=0, keepdims=True).astype(F32))
            a_buf[par, h] = jnp.exp2(m_old - m_new)
            p_buf[par, h] = jnp.exp2(st - m_new.astype(st.dtype)).astype(BF16)
            m_ref[h] = m_new

    def pv(n, par):
        _, j = split(n)
        r0 = _chunk_start(j, tk)
        for h in range(2):
            vc = vt_ref[0, h * VT_ROWS:(h + 1) * VT_ROWS, pl.ds(r0, tk)]
            acc_ref[h] = a_buf[par, h] * acc_ref[h] + _dot(vc, p_buf[par, h])

    def finalize(qi):
        outs = []
        for h in range(2):
            a = acc_ref[h]
            outs.append(a[0:MLA_V] / a[MLA_V:MLA_V + 1])
        ot = jnp.concatenate(outs, axis=0)
        o_ref[pl.ds(_chunk_start(qi, tq), tq), :] = ot.T.astype(o_ref.dtype)

    scores(0, 0)
    scores(1, 1)
    softmax(0, 0)
    trips = (total - 2) // unroll
    assert trips == 0 or (nk % unroll == 0 and nk == 1 << lognk and unroll % 2 == 0)

    def body(i, carry):
        for u in range(unroll):
            n = unroll * i + u
            pv(n, u % 2)
            softmax(n + 1, (u + 1) % 2)
            scores(n + 2, u % 2)
        n_last = unroll * i + unroll - 1

        @pl.when((n_last & (nk - 1)) == nk - 1)
        def _():
            finalize(n_last >> lognk)
        return carry

    lax.fori_loop(0, trips, body, 0)
    for n in range(trips * unroll, total):
        pv(n, n % 2)
        if n + 1 < total:
            softmax(n + 1, (n + 1) % 2)
        if n + 2 < total:
            scores(n + 2, n % 2)
        if n % nk == nk - 1:
            finalize(n // nk)


def _flash(qt, k, vt, *, b, s, tq, tk, unroll=4, s_dtype=BF16):
    nq = s // tq
    nk = s // tk
    hp = MLA_HEADS // 2
    return pl.pallas_call(
        functools.partial(_flash_kernel, tq=tq, tk=tk, nq=nq, nk=nk, unroll=unroll),
        grid=(b, hp),
        in_specs=[pl.BlockSpec((1, 2 * HEAD_SLAB, s), lambda bi, p: (bi, p, 0)),
                  pl.BlockSpec((s, 2 * HEAD_SLAB), lambda bi, p: (bi, p)),
                  pl.BlockSpec((1, 2 * VT_ROWS, s), lambda bi, p: (bi, p, 0))],
        out_specs=pl.BlockSpec((s, 2 * MLA_V), lambda bi, p: (bi, p)),
        out_shape=jax.ShapeDtypeStruct((b * s, MLA_HEADS * MLA_V), BF16),
        scratch_shapes=[pltpu.VMEM((2, 1, tq), F32), pltpu.VMEM((2, VT_ROWS, tq), F32),
                        pltpu.VMEM((2, 2, tk, tq), s_dtype), pltpu.VMEM((2, 2, tk, tq), BF16),
                        pltpu.VMEM((2, 2, 1, tq), F32)],
        compiler_params=_cparams(("parallel", "arbitrary")),
        name="mla_flash",
    )(qt, k, vt)


def _group_sum(z, gmat):
    z1 = z.astype(BF16)
    z2 = (z - z1.astype(F32)).astype(BF16)
    return _dot(z1, gmat) + _dot(z2, gmat)


HALO = 16


def _dn_prep_kernel(x_ref, xp_ref, xn_ref, sm_ref, cw_ref, par_ref, gm_ref,
                    q_out, k_out, v_out, gb_out, pad_ref, *, nb, tm):
    i = pl.program_id(0)
    first = (i % nb) == 0
    last = (i % nb) == nb - 1
    pad_ref[0:HALO, :] = jnp.where(first, 0.0, xp_ref[...].astype(F32))
    pad_ref[HALO:HALO + tm, :] = x_ref[...].astype(F32)
    pad_ref[HALO + tm:2 * HALO + tm, :] = jnp.where(last, 0.0, xn_ref[...].astype(F32))
    cw = cw_ref[...]
    off = HALO - DN_CONV // 2
    y = cw[0:1] * pad_ref[pl.ds(off, tm), :]
    for j in range(1, DN_CONV):
        y = y + cw[j:j + 1] * pad_ref[pl.ds(off + j, tm), :]
    y = y * _sigmoid(y)
    gm = gm_ref[...]
    q = y[:, 0:DN_W]
    k = y[:, DN_W:2 * DN_W]
    q_out[...] = q * lax.rsqrt(_group_sum(q * q, gm) + EPS) * (DN_DK ** -0.5)
    k_out[...] = k * lax.rsqrt(_group_sum(k * k, gm) + EPS)
    v_out[...] = y[:, 2 * DN_W:]

    sm = sm_ref[...]
    par = par_ref[...]
    z = sm + par[1:2]
    softplus = jnp.maximum(z, 0.0) + jnp.log1p(jnp.exp(-jnp.abs(z)))
    g = -jnp.exp(par[0:1]) * softplus
    lane = lax.broadcasted_iota(jnp.int32, sm.shape, 1)
    gb_out[...] = jnp.where(lane < 16, g, _sigmoid(sm))


def _dn_prep(proj, small, conv_w, par, gmat, *, b, s, tm):
    t = b * s
    nb = s // tm
    w3 = 3 * DN_W
    cb = COL_DQKV // w3
    hb = tm // HALO
    last_hb = t // HALO - 1
    return pl.pallas_call(
        functools.partial(_dn_prep_kernel, nb=nb, tm=tm),
        grid=(t // tm,),
        in_specs=[pl.BlockSpec((tm, w3), lambda i: (i, cb)),
                  pl.BlockSpec((HALO, w3), lambda i: (jnp.maximum(i * hb - 1, 0), cb)),
                  pl.BlockSpec((HALO, w3), lambda i: (jnp.minimum((i + 1) * hb, last_hb), cb)),
                  pl.BlockSpec((tm, 128), lambda i: (i, 0)),
                  pl.BlockSpec((8, w3), lambda i: (0, 0)),
                  pl.BlockSpec((8, 128), lambda i: (0, 0)),
                  pl.BlockSpec((DN_W, DN_W), lambda i: (0, 0))],
        out_specs=[pl.BlockSpec((tm, DN_W), lambda i: (i, 0)),
                   pl.BlockSpec((tm, DN_W), lambda i: (i, 0)),
                   pl.BlockSpec((tm, DN_W), lambda i: (i, 0)),
                   pl.BlockSpec((tm, 128), lambda i: (i, 0))],
        out_shape=[jax.ShapeDtypeStruct((t, DN_W), F32)] * 3 + [jax.ShapeDtypeStruct((t, 128), F32)],
        scratch_shapes=[pltpu.VMEM((tm + 2 * HALO, w3), F32)],
        compiler_params=_cparams(("parallel",)),
        name="dn_prep",
    )(proj, proj, proj, small, conv_w, par, gmat)


GRP = 256
NGRP = DN_W // GRP


def _bd_rows(x_bf, bmask):
    return jnp.where(bmask, jnp.concatenate([x_bf] * (GRP // DN_CHUNK), axis=0), jnp.zeros((), BF16))


def _dn_chunk_kernel(qf_ref, kf_ref, vf_ref, gf_ref, qb_ref, kb_ref, vb_ref, gb_ref,
                     of_ref, ob_ref, s_ref, *, nch):
    c = DN_CHUNK

    @pl.when(pl.program_id(1) == 0)
    def _():
        s_ref[...] = jnp.zeros(s_ref.shape, F32)

    ri = lax.broadcasted_iota(jnp.int32, (c, GRP), 0)
    cj = lax.broadcasted_iota(jnp.int32, (c, GRP), 1) & (c - 1)
    br = lax.broadcasted_iota(jnp.int32, (GRP, GRP), 0) >> 6
    bc = lax.broadcasted_iota(jnp.int32, (GRP, GRP), 1) >> 6
    bmask = br == bc
    ident = (ri == cj).astype(F32)
    incl = (ri >= cj, ri <= cj)
    strict = (ri > cj, ri < cj)

    def pmm(a, bmat):
        return _dot(a.astype(BF16), _bd_rows(bmat.astype(BF16), bmask))

    def pmm_nt(a, bmat):
        return _dot_nt(a.astype(BF16), _bd_rows(bmat.astype(BF16), bmask))

    er = lax.broadcasted_iota(jnp.int32, (128, 2 * DN_W), 0)
    ec = lax.broadcasted_iota(jnp.int32, (128, 2 * DN_W), 1) >> 6
    tr = lax.broadcasted_iota(jnp.int32, (c, c), 0)
    tc = lax.broadcasted_iota(jnp.int32, (c, c), 1)
    mr = lax.broadcasted_iota(jnp.int32, (c, DN_W), 0)
    mj = lax.broadcasted_iota(jnp.int32, (c, DN_W), 1) & (c - 1)
    ones_cc = jnp.ones((c, c), BF16)
    tri = ((tc <= tr).astype(BF16), (tc >= tr).astype(BF16))
    mask_t = (mr <= mj, mr >= mj)
    refs = ((qf_ref, kf_ref, vf_ref, gf_ref, of_ref), (qb_ref, kb_ref, vb_ref, gb_ref, ob_ref))

    def rows(d, step):
        i = step if d == 0 else nch - 1 - step
        return slice(i * c, (i + 1) * c)

    def sum3(lhs, rhs3):
        return _dot(lhs, rhs3[0]) + _dot(lhs, rhs3[1]) + _dot(lhs, rhs3[2])

    ds = [(d, step) for step in range(nch) for d in (0, 1)]
    e_gb = [(er == jnp.where(ec < DN_HEADS, d * DN_HEADS + ec, 8 + d * DN_HEADS + ec)).astype(BF16)
            for d in (0, 1)]
    g3 = {u: jnp.concatenate(_split3(refs[u[0]][3][rows(*u), :]), axis=0) for u in ds}
    gbx = {u: _dot(g3[u], e_gb[u[0]]) for u in ds}
    gbx = {u: gbx[u][0:c] + gbx[u][c:2 * c] + gbx[u][2 * c:3 * c] for u in ds}
    gexp = {u: gbx[u][:, :DN_W] for u in ds}
    bexp = {u: gbx[u][:, DN_W:] for u in ds}
    ge3 = {u: _split3(gexp[u]) for u in ds}
    gm3 = {u: _split3(jnp.where(mask_t[u[0]], gexp[u], 0.0)) for u in ds}
    gcrow = {u: sum3(tri[u[0]], ge3[u]) for u in ds}
    gccol = {u: sum3(ones_cc, gm3[u]) for u in ds}

    units = [(d, step, g) for step in range(nch) for d in (0, 1) for g in range(NGRP)]
    sl = lambda g: slice(g * GRP, (g + 1) * GRP)
    kg = {u: refs[u[0]][1][rows(u[0], u[1]), sl(u[2])] for u in units}
    qg = {u: refs[u[0]][0][rows(u[0], u[1]), sl(u[2])] for u in units}
    vg = {u: refs[u[0]][2][rows(u[0], u[1]), sl(u[2])] for u in units}
    bg = {u: bexp[u[:2]][:, sl(u[2])] for u in units}
    gc = {u: gcrow[u[:2]][:, sl(u[2])] for u in units}
    dec = {u: jnp.exp(jnp.where(incl[u[0]], gc[u] - gccol[u[:2]][:, sl(u[2])], -jnp.inf)) for u in units}
    egc = {u: jnp.exp(gc[u]) for u in units}
    kq = {u: pmm_nt(jnp.concatenate([kg[u], qg[u]], axis=0), kg[u]) for u in units}
    kk = {u: kq[u][0:c] for u in units}
    qk = {u: kq[u][c:] for u in units}
    lm = {u: jnp.where(strict[u[0]], kk[u] * bg[u] * dec[u], 0.0) for u in units}

    dd = {u: jnp.where((ri >> 3) == (cj >> 3), lm[u], 0.0) for u in units}
    d2 = {u: pmm(dd[u], dd[u]) for u in units}
    d4 = {u: pmm(d2[u], d2[u]) for u in units}
    x = {u: pmm(ident - dd[u], ident + d2[u]) for u in units}
    x = {u: pmm(x[u], ident + d4[u]) for u in units}
    for sh in (3, 4, 5):
        off = ((ri >> (sh + 1)) == (cj >> (sh + 1))) & ((ri >> sh) != (cj >> sh))
        xm = {u: pmm(x[u], jnp.where(off, lm[u], 0.0)) for u in units}
        x = {u: x[u] - pmm(xm[u], x[u]) for u in units}
    uu = {u: pmm(x[u], vg[u] * bg[u]) for u in units}
    ww = {u: pmm(x[u], kg[u] * bg[u] * egc[u]).astype(BF16) for u in units}
    att = {u: jnp.where(incl[u[0]], qk[u] * dec[u], 0.0).astype(BF16) for u in units}
    wq = {u: jnp.concatenate([ww[u], (qg[u] * egc[u]).astype(BF16)], axis=0) for u in units}
    gl = {u: (gc[u][c - 1:c] if u[0] == 0 else gc[u][0:1]) for u in units}
    kdec = {u: (kg[u] * jnp.exp(gl[u] - gc[u])).astype(BF16) for u in units}
    egl = {u: jnp.exp(gl[u]) for u in units}

    for step in range(nch):
        cur = [(d, step, g) for d in (0, 1) for g in range(NGRP)]
        st = {u: s_ref[u[0] * NGRP + u[2]] for u in cur}
        ws = {u: _dot(wq[u], st[u].astype(BF16)) for u in cur}
        vnew = {u: uu[u] - ws[u][0:c] for u in cur}
        vbd = {u: _bd_rows(vnew[u].astype(BF16), bmask) for u in cur}
        for u in cur:
            refs[u[0]][4][rows(u[0], step), sl(u[2])] = ws[u][c:] + _dot(att[u], vbd[u])
        upd = {u: _dot_tn(kdec[u], vnew[u].astype(BF16)) for u in cur}
        for u in cur:
            s_ref[u[0] * NGRP + u[2]] = st[u] * egl[u] + jnp.where(bmask, upd[u], 0.0)


def _dn_chunk(qn, kn, vv, gb, *, b, s, nch):
    n = s // (DN_CHUNK * nch)
    t = b * s
    fwd = lambda bi, ci: (bi * n + ci, 0)
    bwd = lambda bi, ci: (bi * n + n - 1 - ci, 0)
    wide = lambda im: pl.BlockSpec((DN_CHUNK * nch, DN_W), im)
    small = lambda im: pl.BlockSpec((DN_CHUNK * nch, 128), im)
    return pl.pallas_call(
        functools.partial(_dn_chunk_kernel, nch=nch),
        grid=(b, n),
        in_specs=[wide(fwd), wide(fwd), wide(fwd), small(fwd), wide(bwd), wide(bwd), wide(bwd), small(bwd)],
        out_specs=[wide(fwd), wide(bwd)],
        out_shape=[jax.ShapeDtypeStruct((t, DN_W), F32)] * 2,
        scratch_shapes=[pltpu.VMEM((2 * NGRP, GRP, GRP), F32)],
        compiler_params=_cparams(("parallel", "arbitrary")),
        name="dn_chunk",
    )(qn, kn, vv, gb, qn, kn, vv, gb)


def _merge_kernel(x_ref, gates_ref, dz_ref, mq_ref, omla_ref, of_ref, ob_ref, mk_ref, mv_ref,
                  onorm_ref, gm_ref, wm_ref, wd_ref, wmem_ref, wout_ref, o_ref):
    o = of_ref[...] + ob_ref[...]
    ms = _group_sum(o * o, gm_ref[...]) * (1.0 / DN_DK)
    dz = dz_ref[...].astype(F32)
    o_dn = o * lax.rsqrt(ms + EPS) * onorm_ref[...] * (dz * _sigmoid(dz))

    mq = mq_ref[...]
    mk = mk_ref[0]
    mv = mv_ref[0]
    outs = []
    for h in range(MEM_HEADS):
        hs = slice(h * MEM_HD, (h + 1) * MEM_HD)
        sc = _dot_nt(mq[:, hs], mk[:, hs]) * (MEM_HD ** -0.5)
        p = jnp.exp(sc - jnp.max(sc, axis=-1, keepdims=True))
        p = p / jnp.sum(p, axis=-1, keepdims=True)
        outs.append(_dot(p.astype(BF16), mv[:, hs]))
    o_mem = jnp.concatenate(outs, axis=-1)

    gates = gates_ref[...].astype(F32)
    merged = (_sigmoid(gates[:, 0:D_MODEL]) * _dot(omla_ref[...], wm_ref[...])
              + _sigmoid(gates[:, D_MODEL:2 * D_MODEL]) * _dot(o_dn.astype(BF16), wd_ref[...])
              + _sigmoid(gates[:, 2 * D_MODEL:]) * _dot(o_mem.astype(BF16), wmem_ref[...]))
    o_ref[...] = x_ref[...] + _dot(merged.astype(BF16), wout_ref[...])


def _merge(x, proj, omla, o_f, o_b, memkv, onorm, gmat, wm, wd, wmem, wout, *, b, s, tm):
    t = b * s
    nb = s // tm
    mt = memkv.shape[1]
    const = lambda shape: pl.BlockSpec(shape, lambda i: (0,) * len(shape))
    return pl.pallas_call(
        _merge_kernel,
        grid=(t // tm,),
        in_specs=[pl.BlockSpec((tm, D_MODEL), lambda i: (i, 0)),
                  pl.BlockSpec((tm, 3 * D_MODEL), lambda i: (i, COL_GATES // (3 * D_MODEL))),
                  pl.BlockSpec((tm, DN_W), lambda i: (i, COL_DZ // DN_W)),
                  pl.BlockSpec((tm, MEM_W), lambda i: (i, COL_MQ // MEM_W)),
                  pl.BlockSpec((tm, DN_W), lambda i: (i, 0)),
                  pl.BlockSpec((tm, DN_W), lambda i: (i, 0)),
                  pl.BlockSpec((tm, DN_W), lambda i: (i, 0)),
                  pl.BlockSpec((1, mt, MEM_W), lambda i: (i // nb, 0, 0)),
                  pl.BlockSpec((1, mt, MEM_W), lambda i: (i // nb, 0, 1)),
                  const((1, DN_W)), const((DN_W, DN_W)),
                  const((DN_W, D_MODEL)), const((DN_W, D_MODEL)), const((MEM_W, D_MODEL)),
                  const((D_MODEL, D_MODEL))],
        out_specs=pl.BlockSpec((tm, D_MODEL), lambda i: (i, 0)),
        out_shape=jax.ShapeDtypeStruct((t, D_MODEL), F32),
        compiler_params=_cparams(("parallel",)),
        name="merge",
    )(x, proj, proj, proj, omla, o_f, o_b, memkv, memkv, onorm, gmat, wm, wd, wmem, wout)


def _ffn_kernel(x_ref, nw_ref, wgu_ref, wd_ref, fw_ref, o_ref, *, nf, final_norm):
    x = x_ref[...]
    h = _rms(x, nw_ref[...]).astype(BF16)
    tf = D_FF // nf
    y = x
    for c in range(nf):
        gate = _dot(h, wgu_ref[:, c * tf:(c + 1) * tf])
        up = _dot(h, wgu_ref[:, D_FF + c * tf:D_FF + (c + 1) * tf])
        act = (gate * _sigmoid(gate) * up).astype(BF16)
        y = y + _dot(act, wd_ref[c * tf:(c + 1) * tf, :])
    if final_norm:
        y = _rms(y, fw_ref[...])
    o_ref[...] = y


def _ffn(x, nw, wgu, wd, fw, *, tm, nf, final_norm):
    t = x.shape[0]
    resident = lambda shape: pl.BlockSpec(shape, lambda i: (0, 0), pipeline_mode=pl.Buffered(1))
    return pl.pallas_call(
        functools.partial(_ffn_kernel, nf=nf, final_norm=final_norm),
        grid=(t // tm,),
        in_specs=[pl.BlockSpec((tm, D_MODEL), lambda i: (i, 0)),
                  pl.BlockSpec((1, D_MODEL), lambda i: (0, 0)),
                  resident((D_MODEL, 2 * D_FF)),
                  resident((D_FF, D_MODEL)),
                  pl.BlockSpec((1, D_MODEL), lambda i: (0, 0))],
        out_specs=pl.BlockSpec((tm, D_MODEL), lambda i: (i, 0)),
        out_shape=jax.ShapeDtypeStruct((t, D_MODEL), F32),
        compiler_params=_cparams(("parallel",)),
        name="ffn",
    )(x, nw, wgu, wd, fw)


def _pick(n, pref):
    return pref if n % pref == 0 else n


def _rope_tables(positions):
    b, s = positions.shape
    half = MLA_ROPE // 2
    inv_freq = 1.0 / (ROPE_THETA ** (jnp.arange(0, MLA_ROPE, 2, dtype=F32) / MLA_ROPE))
    ang = (positions.astype(F32)[..., None] * inv_freq).reshape(-1, 128)
    cos, sin = lax.optimization_barrier((jnp.cos(ang), jnp.sin(ang)))
    cos = cos.reshape(b * s, half)
    sin = sin.reshape(b * s, half)
    z64 = jnp.zeros((b * s, MLA_NOPE), F32)
    z32 = jnp.zeros((b * s, 32), F32)
    ck = jnp.concatenate([z64, cos, cos, z32], axis=1)
    sk = jnp.concatenate([z64, sin, sin, z32], axis=1)
    qscale = (MLA_QK ** -0.5) * LOG2E
    cq = jnp.concatenate([jnp.ones_like(z64), cos, cos, z32], axis=1) * qscale
    sq = sk * qscale
    to_t = lambda a: a.reshape(b, s, 128).transpose(0, 2, 1)
    return ck, sk, to_t(cq), to_t(sq)


def _layer_weights(w_in, w_uq, w_ukv):
    d = w_in.shape[0]
    w_in = w_in.astype(BF16)
    z = lambda n: jnp.zeros((d, n), BF16)
    kr = w_in[:, 512:544]
    w_re = jnp.concatenate([
        w_in[:, 3136:6208], w_in[:, 544:2080], w_in[:, 2080:2592], w_in[:, 2624:3136],
        w_in[:, 0:256], w_in[:, 256:512],
        w_in[:, 2592:2624], z(32), kr, z(32),
        z(64), -kr[:, 16:32], kr[:, 0:16], z(32)], axis=1)

    uq = w_uq.reshape(MLA_Q_LORA, MLA_HEADS, MLA_QK)
    zq = lambda n: jnp.zeros((MLA_Q_LORA, MLA_HEADS, n), w_uq.dtype)
    x1, x2 = uq[..., 64:80], uq[..., 80:96]
    wq1 = jnp.concatenate([uq[..., :64], x1, x2, zq(32)], axis=-1).reshape(MLA_Q_LORA, 1024)
    wq2 = jnp.concatenate([zq(64), -x2, x1, zq(32)], axis=-1).reshape(MLA_Q_LORA, 1024)
    ukv = w_ukv.reshape(MLA_KV_LORA, MLA_HEADS, MLA_NOPE + MLA_V)
    wk = jnp.concatenate([ukv[..., :64], jnp.zeros((MLA_KV_LORA, MLA_HEADS, 64), w_ukv.dtype)],
                         axis=-1).reshape(MLA_KV_LORA, 1024)
    wv = ukv[..., 64:].reshape(MLA_KV_LORA, MLA_HEADS * MLA_V)
    return w_re, wq1.T.astype(BF16), wq2.T.astype(BF16), wk.astype(BF16), wv.T.astype(BF16)


def kernel(x, mem, positions, norm_mix, w_in, mla_q_norm, mla_w_uq, mla_kv_norm, mla_w_ukv, dn_conv, dn_a_log, dn_dt_bias, dn_out_norm, mem_norm, mem_w_kv, w_branch_mla, w_branch_dn, w_branch_mem, w_out, norm_ffn, ffn_w_gate_up, ffn_w_down, final_norm):
    b, s, d = x.shape
    t = b * s
    depth = w_in.shape[0]
    mt = mem.shape[1]
    ck, sk, cqt, sqt = _rope_tables(positions)
    gi = jnp.arange(DN_W) // DN_DK
    gmat = (gi[:, None] == gi[None, :]).astype(BF16)
    xf = x.reshape(t, d)
    memf = mem.reshape(b * mt, d)
    tm = _pick(s, 512)
    for l in range(depth):
        w_re, wq1t, wq2t, wk, wvt = _layer_weights(w_in[l], mla_w_uq[l], mla_w_ukv[l])
        proj, small = _norm_matmul(xf, norm_mix[l][None], w_re, tm=_pick(t, 2048), tn=1280, tail=IN_COLS_P - COL_SMALL)

        qt, kk, vt = _mla_prep(proj, small, ck, sk, cqt, sqt, mla_q_norm[l][None], mla_kv_norm[l][None],
                               wq1t, wq2t, wk, wvt, b=b, s=s, tm=tm)
        o_mla = _flash(qt, kk, vt, b=b, s=s, tq=_pick(s, 512), tk=_pick(s, 256), unroll=min(32, s // _pick(s, 256)))

        conv_w = jnp.concatenate([dn_conv[l], jnp.zeros((8 - DN_CONV, 3 * DN_W), F32)], axis=0)
        pad16 = lambda v: jnp.concatenate([v.reshape(-1), jnp.zeros((128 - 2 * DN_HEADS,), F32)])
        par = jnp.concatenate([pad16(dn_a_log[l])[None], pad16(dn_dt_bias[l])[None],
                               jnp.zeros((6, 128), F32)], axis=0)
        qn, kn, vv, gb = _dn_prep(proj, small, conv_w, par, gmat, b=b, s=s, tm=tm)
        o_f, o_b = _dn_chunk(qn, kn, vv, gb, b=b, s=s, nch=4)
        n_chunks = s // DN_CHUNK
        perm = lambda a: a.reshape(b, n_chunks, DN_CHUNK, DN_W).swapaxes(1, 2).reshape(t, DN_W)
        o_f, o_b = perm(o_f), perm(o_b)

        memkv = _norm_matmul(memf, mem_norm[l][None], mem_w_kv[l].astype(BF16),
                             tm=b * mt, tn=512)[0].reshape(b, mt, 2 * MEM_W)
        onorm = jnp.tile(dn_out_norm[l], DN_HEADS)[None]
        xf = _merge(xf, proj, o_mla, o_f, o_b, memkv, onorm, gmat,
                    w_branch_mla[l].astype(BF16), w_branch_dn[l].astype(BF16),
                    w_branch_mem[l].astype(BF16), w_out[l].astype(BF16), b=b, s=s, tm=tm)
        xf = _ffn(xf, norm_ffn[l][None], ffn_w_gate_up[l].astype(BF16), ffn_w_down[l].astype(BF16),
                  final_norm[None], tm=_pick(s, 1024), nf=11, final_norm=(l == depth - 1))
    return xf.reshape(b, s, d)
```

```python
import functools

import jax
import jax.numpy as jnp
from jax import lax
from jax.experimental import pallas as pl
from jax.experimental.pallas import tpu as pltpu

F32 = jnp.float32
BF16 = jnp.bfloat16

D_MODEL = 1024
EPS = 1e-6
MLA_HEADS = 8
MLA_Q_LORA = 256
MLA_KV_LORA = 256
MLA_NOPE = 64
MLA_ROPE = 32
MLA_V = 64
MLA_QK = MLA_NOPE + MLA_ROPE
ROPE_THETA = 10000.0
DN_HEADS = 8
DN_DK = 64
DN_W = DN_HEADS * DN_DK
DN_CONV = 5
DN_CHUNK = 64
MEM_HEADS = 4
MEM_HD = 128
MEM_W = MEM_HEADS * MEM_HD
D_FF = 2816

COL_GATES = 0
COL_DQKV = 3072
COL_DZ = 4608
COL_MQ = 5120
COL_CQ = 5632
COL_CKV = 5888
COL_SMALL = 6144
COL_SMALL2 = 6272
IN_COLS_P = 6400

HEAD_SLAB = 128
VT_ROWS = 80
LOG2E = 1.4426950408889634
VMEM_LIMIT = 48 * 1024 * 1024


def _cparams(sem):
    return pltpu.CompilerParams(dimension_semantics=sem, vmem_limit_bytes=VMEM_LIMIT)


def _sigmoid(x):
    return 0.5 * jnp.tanh(0.5 * x) + 0.5


def _dot(a, b):
    return jnp.dot(a, b, preferred_element_type=F32)


def _dot_nt(a, b):
    return lax.dot_general(a, b, (((1,), (1,)), ((), ())), preferred_element_type=F32)


def _dot_tn(a, b):
    return lax.dot_general(a, b, (((0,), (0,)), ((), ())), preferred_element_type=F32)


def _split3(x):
    x1 = x.astype(BF16)
    r1 = x - x1.astype(F32)
    x2 = r1.astype(BF16)
    x3 = (r1 - x2.astype(F32)).astype(BF16)
    return x1, x2, x3


def _rms(x, w):
    return x * lax.rsqrt(jnp.mean(x * x, axis=-1, keepdims=True) + EPS) * w


def _norm_matmul_kernel(x_ref, g_ref, w_ref, o_ref, *rest, tail):
    h_ref = rest[-1]

    @pl.when(pl.program_id(1) == 0)
    def _():
        h_ref[...] = _rms(x_ref[...], g_ref[...]).astype(BF16)

    acc = _dot(h_ref[...], w_ref[...])
    o_ref[...] = acc.astype(o_ref.dtype)
    if tail:
        @pl.when(pl.program_id(1) == pl.num_programs(1) - 1)
        def _():
            rest[0][...] = acc[:, acc.shape[1] - tail:]


def _norm_matmul(x, g, w, *, layer, tm, tn, tail=0):
    t, d = x.shape
    n = w.shape[2]
    out_specs = [pl.BlockSpec((tm, tn), lambda i, j: (i, j))]
    out_shape = [jax.ShapeDtypeStruct((t, n), BF16)]
    if tail:
        out_specs.append(pl.BlockSpec((tm, tail), lambda i, j: (i, 0)))
        out_shape.append(jax.ShapeDtypeStruct((t, tail), F32))
    return pl.pallas_call(
        functools.partial(_norm_matmul_kernel, tail=tail),
        grid=(t // tm, n // tn),
        in_specs=[pl.BlockSpec((tm, d), lambda i, j: (i, 0)),
                  pl.BlockSpec((1, d), lambda i, j: (0, 0)),
                  pl.BlockSpec((None, d, tn), lambda i, j: (layer, 0, j))],
        out_specs=out_specs,
        out_shape=out_shape,
        scratch_shapes=[pltpu.VMEM((tm, d), BF16)],
        compiler_params=_cparams(("parallel", "arbitrary")),
        name="norm_matmul",
    )(x, g, w)


def _mla_prep_kernel(cq_ref, ckv_ref, sm_ref, ck_ref, sk_ref, cqt_ref, sqt_ref, qn_ref, kvn_ref,
                     wq1t_ref, wq2t_ref, wk_ref, wvt_ref, qt_out, k_out, vt_out):
    qn = _rms(cq_ref[...].astype(F32), qn_ref[...]).astype(BF16)
    q1t = _dot_nt(wq1t_ref[...], qn)
    q2t = _dot_nt(wq2t_ref[...], qn)
    ct = cqt_ref[0]
    st = sqt_ref[0]
    for h in range(MLA_HEADS):
        hs = slice(h * HEAD_SLAB, (h + 1) * HEAD_SLAB)
        qt_out[0, hs, :] = (q1t[hs] * ct + q2t[hs] * st).astype(BF16)

    kvn = _rms(ckv_ref[...].astype(F32), kvn_ref[...]).astype(BF16)
    kk = _dot(kvn, wk_ref[...])
    sm = sm_ref[...]
    kr = sm[:, :128] * ck_ref[...] + sm[:, 128:] * sk_ref[...]
    for h in range(MLA_HEADS):
        hs = slice(h * HEAD_SLAB, (h + 1) * HEAD_SLAB)
        k_out[:, hs] = (kk[:, hs] + kr).astype(BF16)

    vt = _dot_nt(wvt_ref[...], kvn)
    tm = vt.shape[1]
    ones = jnp.ones((VT_ROWS - MLA_V, tm), BF16)
    for h in range(MLA_HEADS):
        vt_out[0, h * VT_ROWS:h * VT_ROWS + MLA_V, :] = vt[h * MLA_V:(h + 1) * MLA_V].astype(BF16)
        vt_out[0, h * VT_ROWS + MLA_V:(h + 1) * VT_ROWS, :] = ones


def _mla_prep(proj, small, ck, sk, cqt, sqt, qnorm, kvnorm, wq1t, wq2t, wk, wvt, *, b, s, tm):
    nb = s // tm
    t = b * s
    row = lambda bi, i: bi * nb + i
    const = lambda shape: pl.BlockSpec(shape, lambda bi, i: (0,) * len(shape))
    return pl.pallas_call(
        _mla_prep_kernel,
        grid=(b, nb),
        in_specs=[pl.BlockSpec((tm, 256), lambda bi, i: (row(bi, i), COL_CQ // 256)),
                  pl.BlockSpec((tm, 256), lambda bi, i: (row(bi, i), COL_CKV // 256)),
                  pl.BlockSpec((tm, 256), lambda bi, i: (row(bi, i), 0)),
                  pl.BlockSpec((tm, 128), lambda bi, i: (row(bi, i), 0)),
                  pl.BlockSpec((tm, 128), lambda bi, i: (row(bi, i), 0)),
                  pl.BlockSpec((1, 128, tm), lambda bi, i: (bi, 0, i)),
                  pl.BlockSpec((1, 128, tm), lambda bi, i: (bi, 0, i)),
                  const((1, 256)), const((1, 256)),
                  const((1024, 256)), const((1024, 256)), const((256, 1024)), const((512, 256))],
        out_specs=[pl.BlockSpec((1, 1024, tm), lambda bi, i: (bi, 0, i)),
                   pl.BlockSpec((tm, 1024), lambda bi, i: (row(bi, i), 0)),
                   pl.BlockSpec((1, MLA_HEADS * VT_ROWS, tm), lambda bi, i: (bi, 0, i))],
        out_shape=[jax.ShapeDtypeStruct((b, 1024, s), BF16),
                   jax.ShapeDtypeStruct((t, 1024), BF16),
                   jax.ShapeDtypeStruct((b, MLA_HEADS * VT_ROWS, s), BF16)],
        compiler_params=_cparams(("parallel", "parallel")),
        name="mla_prep",
    )(proj, proj, small, ck, sk, cqt, sqt, qnorm, kvnorm, wq1t, wq2t, wk, wvt)


def _chunk_start(j, tk):
    return j * tk if isinstance(j, int) else pl.multiple_of(j * tk, tk)


def _flash_kernel(qt_ref, k_ref, vt_ref, o_ref, m_ref, acc_ref, s_buf, p_buf, a_buf, *, tq, tk, nq, nk, unroll):
    total = nq * nk
    lognk = nk.bit_length() - 1
    m_ref[...] = jnp.full(m_ref.shape, -jnp.inf, F32)
    acc_ref[...] = jnp.zeros(acc_ref.shape, F32)

    def split(n):
        if isinstance(n, int):
            return n // nk, n % nk
        return n >> lognk, n & (nk - 1)

    def scores(n, par):
        qi, j = split(n)
        q0 = _chunk_start(qi, tq)
        r0 = _chunk_start(j, tk)
        for h in range(2):
            hs = slice(h * HEAD_SLAB, (h + 1) * HEAD_SLAB)
            s_buf[par, h] = _dot(k_ref[pl.ds(r0, tk), hs],
                                 qt_ref[0, hs, pl.ds(q0, tq)]).astype(s_buf.dtype)

    def softmax(n, par):
        _, j = split(n)
        for h in range(2):
            st = s_buf[par, h]
            m_old = jnp.where(j == 0, -jnp.inf, m_ref[h])
            m_new = jnp.maximum(m_old, jnp.max(st, axis=0, keepdims=True).astype(F32))
            a_buf[par, h] = jnp.exp2(m_old - m_new)
            p_buf[par, h] = jnp.exp2(st - m_new.astype(st.dtype)).astype(BF16)
            m_ref[h] = m_new

    def pv(n, par):
        _, j = split(n)
        r0 = _chunk_start(j, tk)
        for h in range(2):
            vc = vt_ref[0, h * VT_ROWS:(h + 1) * VT_ROWS, pl.ds(r0, tk)]
            acc_ref[h] = a_buf[par, h] * acc_ref[h] + _dot(vc, p_buf[par, h])

    def finalize(qi):
        outs = []
        for h in range(2):
            a = acc_ref[h]
            outs.append(a[0:MLA_V] / a[MLA_V:MLA_V + 1])
        ot = jnp.concatenate(outs, axis=0)
        o_ref[pl.ds(_chunk_start(qi, tq), tq), :] = ot.T.astype(o_ref.dtype)

    scores(0, 0)
    scores(1, 1)
    softmax(0, 0)
    trips = (total - 2) // unroll
    assert trips == 0 or (nk % unroll == 0 and nk == 1 << lognk and unroll % 2 == 0)

    def body(i, carry):
        for u in range(unroll):
            n = unroll * i + u
            pv(n, u % 2)
            softmax(n + 1, (u + 1) % 2)
            scores(n + 2, u % 2)
        n_last = unroll * i + unroll - 1

        @pl.when((n_last & (nk - 1)) == nk - 1)
        def _():
            finalize(n_last >> lognk)
        return carry

    lax.fori_loop(0, trips, body, 0)
    for n in range(trips * unroll, total):
        pv(n, n % 2)
        if n + 1 < total:
            softmax(n + 1, (n + 1) % 2)
        if n + 2 < total:
            scores(n + 2, n % 2)
        if n % nk == nk - 1:
            finalize(n // nk)


def _flash(qt, k, vt, *, b, s, tq, tk, unroll=4, s_dtype=BF16):
    nq = s // tq
    nk = s // tk
    hp = MLA_HEADS // 2
    return pl.pallas_call(
        functools.partial(_flash_kernel, tq=tq, tk=tk, nq=nq, nk=nk, unroll=unroll),
        grid=(b, hp),
        in_specs=[pl.BlockSpec((1, 2 * HEAD_SLAB, s), lambda bi, p: (bi, p, 0)),
                  pl.BlockSpec((s, 2 * HEAD_SLAB), lambda bi, p: (bi, p)),
                  pl.BlockSpec((1, 2 * VT_ROWS, s), lambda bi, p: (bi, p, 0))],
        out_specs=pl.BlockSpec((s, 2 * MLA_V), lambda bi, p: (bi, p)),
        out_shape=jax.ShapeDtypeStruct((b * s, MLA_HEADS * MLA_V), BF16),
        scratch_shapes=[pltpu.VMEM((2, 1, tq), F32), pltpu.VMEM((2, VT_ROWS, tq), F32),
                        pltpu.VMEM((2, 2, tk, tq), s_dtype), pltpu.VMEM((2, 2, tk, tq), BF16),
                        pltpu.VMEM((2, 2, 1, tq), F32)],
        compiler_params=_cparams(("parallel", "arbitrary")),
        name="mla_flash",
    )(qt, k, vt)


def _group_sum(z, gmat):
    z1 = z.astype(BF16)
    z2 = (z - z1.astype(F32)).astype(BF16)
    return _dot(z1, gmat) + _dot(z2, gmat)


HALO = 16


def _dn_prep_kernel(x_ref, xp_ref, xn_ref, sm_ref, cw_ref, par_ref, gm_ref,
                    q_out, k_out, v_out, gb_out, pad_ref, *, nb, tm):
    i = pl.program_id(0)
    first = (i % nb) == 0
    last = (i % nb) == nb - 1
    pad_ref[0:HALO, :] = jnp.where(first, 0.0, xp_ref[...].astype(F32))
    pad_ref[HALO:HALO + tm, :] = x_ref[...].astype(F32)
    pad_ref[HALO + tm:2 * HALO + tm, :] = jnp.where(last, 0.0, xn_ref[...].astype(F32))
    cw = cw_ref[...]
    off = HALO - DN_CONV // 2
    y = cw[0:1] * pad_ref[pl.ds(off, tm), :]
    for j in range(1, DN_CONV):
        y = y + cw[j:j + 1] * pad_ref[pl.ds(off + j, tm), :]
    y = y * _sigmoid(y)
    gm = gm_ref[...]
    q = y[:, 0:DN_W]
    k = y[:, DN_W:2 * DN_W]
    q_out[...] = q * lax.rsqrt(_group_sum(q * q, gm) + EPS) * (DN_DK ** -0.5)
    k_out[...] = k * lax.rsqrt(_group_sum(k * k, gm) + EPS)
    v_out[...] = y[:, 2 * DN_W:]

    sm = sm_ref[...]
    par = par_ref[...]
    z = sm + par[1:2]
    softplus = jnp.maximum(z, 0.0) + jnp.log1p(jnp.exp(-jnp.abs(z)))
    g = -jnp.exp(par[0:1]) * softplus
    lane = lax.broadcasted_iota(jnp.int32, sm.shape, 1)
    gb_out[...] = jnp.where(lane < 16, g, _sigmoid(sm))


def _dn_prep(proj, small, conv_w, par, gmat, *, b, s, tm):
    t = b * s
    nb = s // tm
    w3 = 3 * DN_W
    cb = COL_DQKV // w3
    hb = tm // HALO
    last_hb = t // HALO - 1
    return pl.pallas_call(
        functools.partial(_dn_prep_kernel, nb=nb, tm=tm),
        grid=(t // tm,),
        in_specs=[pl.BlockSpec((tm, w3), lambda i: (i, cb)),
                  pl.BlockSpec((HALO, w3), lambda i: (jnp.maximum(i * hb - 1, 0), cb)),
                  pl.BlockSpec((HALO, w3), lambda i: (jnp.minimum((i + 1) * hb, last_hb), cb)),
                  pl.BlockSpec((tm, 128), lambda i: (i, 0)),
                  pl.BlockSpec((8, w3), lambda i: (0, 0)),
                  pl.BlockSpec((8, 128), lambda i: (0, 0)),
                  pl.BlockSpec((DN_W, DN_W), lambda i: (0, 0))],
        out_specs=[pl.BlockSpec((tm, DN_W), lambda i: (i, 0)),
                   pl.BlockSpec((tm, DN_W), lambda i: (i, 0)),
                   pl.BlockSpec((tm, DN_W), lambda i: (i, 0)),
                   pl.BlockSpec((tm, 128), lambda i: (i, 0))],
        out_shape=[jax.ShapeDtypeStruct((t, DN_W), F32)] * 3 + [jax.ShapeDtypeStruct((t, 128), F32)],
        scratch_shapes=[pltpu.VMEM((tm + 2 * HALO, w3), F32)],
        compiler_params=_cparams(("parallel",)),
        name="dn_prep",
    )(proj, proj, proj, small, conv_w, par, gmat)


GRP = 256
NGRP = DN_W // GRP


def _bd_rows(x_bf, bmask):
    return jnp.where(bmask, jnp.concatenate([x_bf] * (GRP // DN_CHUNK), axis=0), jnp.zeros((), BF16))


def _dn_chunk_kernel(qf_ref, kf_ref, vf_ref, gf_ref, qb_ref, kb_ref, vb_ref, gb_ref,
                     of_ref, ob_ref, s_ref, *, nch):
    c = DN_CHUNK

    @pl.when(pl.program_id(1) == 0)
    def _():
        s_ref[...] = jnp.zeros(s_ref.shape, F32)

    ri = lax.broadcasted_iota(jnp.int32, (c, GRP), 0)
    cj = lax.broadcasted_iota(jnp.int32, (c, GRP), 1) & (c - 1)
    br = lax.broadcasted_iota(jnp.int32, (GRP, GRP), 0) >> 6
    bc = lax.broadcasted_iota(jnp.int32, (GRP, GRP), 1) >> 6
    bmask = br == bc
    ident = (ri == cj).astype(F32)
    incl = (ri >= cj, ri <= cj)
    strict = (ri > cj, ri < cj)

    def pmm(a, bmat):
        return _dot(a.astype(BF16), _bd_rows(bmat.astype(BF16), bmask))

    def pmm_nt(a, bmat):
        return _dot_nt(a.astype(BF16), _bd_rows(bmat.astype(BF16), bmask))

    er = lax.broadcasted_iota(jnp.int32, (128, 2 * DN_W), 0)
    ec = lax.broadcasted_iota(jnp.int32, (128, 2 * DN_W), 1) >> 6
    tr = lax.broadcasted_iota(jnp.int32, (c, c), 0)
    tc = lax.broadcasted_iota(jnp.int32, (c, c), 1)
    mr = lax.broadcasted_iota(jnp.int32, (c, DN_W), 0)
    mj = lax.broadcasted_iota(jnp.int32, (c, DN_W), 1) & (c - 1)
    ones_cc = jnp.ones((c, c), BF16)
    tri = ((tc <= tr).astype(BF16), (tc >= tr).astype(BF16))
    mask_t = (mr <= mj, mr >= mj)
    refs = ((qf_ref, kf_ref, vf_ref, gf_ref, of_ref), (qb_ref, kb_ref, vb_ref, gb_ref, ob_ref))

    def rows(d, step):
        i = step if d == 0 else nch - 1 - step
        return slice(i * c, (i + 1) * c)

    def sum3(lhs, rhs3):
        return _dot(lhs, rhs3[0]) + _dot(lhs, rhs3[1]) + _dot(lhs, rhs3[2])

    ds = [(d, step) for step in range(nch) for d in (0, 1)]
    e_gb = [(er == jnp.where(ec < DN_HEADS, d * DN_HEADS + ec, 8 + d * DN_HEADS + ec)).astype(BF16)
            for d in (0, 1)]
    g3 = {u: jnp.concatenate(_split3(refs[u[0]][3][rows(*u), :]), axis=0) for u in ds}
    gbx = {u: _dot(g3[u], e_gb[u[0]]) for u in ds}
    gbx = {u: gbx[u][0:c] + gbx[u][c:2 * c] + gbx[u][2 * c:3 * c] for u in ds}
    gexp = {u: gbx[u][:, :DN_W] for u in ds}
    bexp = {u: gbx[u][:, DN_W:] for u in ds}
    ge3 = {u: _split3(gexp[u]) for u in ds}
    gm3 = {u: _split3(jnp.where(mask_t[u[0]], gexp[u], 0.0)) for u in ds}
    gcrow = {u: sum3(tri[u[0]], ge3[u]) for u in ds}
    gccol = {u: sum3(ones_cc, gm3[u]) for u in ds}

    units = [(d, step, g) for step in range(nch) for d in (0, 1) for g in range(NGRP)]
    sl = lambda g: slice(g * GRP, (g + 1) * GRP)
    kg = {u: refs[u[0]][1][rows(u[0], u[1]), sl(u[2])] for u in units}
    qg = {u: refs[u[0]][0][rows(u[0], u[1]), sl(u[2])] for u in units}
    vg = {u: refs[u[0]][2][rows(u[0], u[1]), sl(u[2])] for u in units}
    bg = {u: bexp[u[:2]][:, sl(u[2])] for u in units}
    gc = {u: gcrow[u[:2]][:, sl(u[2])] for u in units}
    dec = {u: jnp.exp(jnp.where(incl[u[0]], gc[u] - gccol[u[:2]][:, sl(u[2])], -jnp.inf)) for u in units}
    egc = {u: jnp.exp(gc[u]) for u in units}
    kq = {u: pmm_nt(jnp.concatenate([kg[u], qg[u]], axis=0), kg[u]) for u in units}
    kk = {u: kq[u][0:c] for u in units}
    qk = {u: kq[u][c:] for u in units}
    lm = {u: jnp.where(strict[u[0]], kk[u] * bg[u] * dec[u], 0.0) for u in units}

    dd = {u: jnp.where((ri >> 3) == (cj >> 3), lm[u], 0.0) for u in units}
    d2 = {u: pmm(dd[u], dd[u]) for u in units}
    d4 = {u: pmm(d2[u], d2[u]) for u in units}
    x = {u: pmm(ident - dd[u], ident + d2[u]) for u in units}
    x = {u: pmm(x[u], ident + d4[u]) for u in units}
    for sh in (3, 4, 5):
        off = ((ri >> (sh + 1)) == (cj >> (sh + 1))) & ((ri >> sh) != (cj >> sh))
        xm = {u: pmm(x[u], jnp.where(off, lm[u], 0.0)) for u in units}
        x = {u: x[u] - pmm(xm[u], x[u]) for u in units}
    uu = {u: pmm(x[u], vg[u] * bg[u]) for u in units}
    ww = {u: pmm(x[u], kg[u] * bg[u] * egc[u]).astype(BF16) for u in units}
    att = {u: jnp.where(incl[u[0]], qk[u] * dec[u], 0.0).astype(BF16) for u in units}
    wq = {u: jnp.concatenate([ww[u], (qg[u] * egc[u]).astype(BF16)], axis=0) for u in units}
    gl = {u: (gc[u][c - 1:c] if u[0] == 0 else gc[u][0:1]) for u in units}
    kdec = {u: (kg[u] * jnp.exp(gl[u] - gc[u])).astype(BF16) for u in units}
    egl = {u: jnp.exp(gl[u]) for u in units}

    for step in range(nch):
        cur = [(d, step, g) for d in (0, 1) for g in range(NGRP)]
        st = {u: s_ref[u[0] * NGRP + u[2]] for u in cur}
        ws = {u: _dot(wq[u], st[u].astype(BF16)) for u in cur}
        vnew = {u: uu[u] - ws[u][0:c] for u in cur}
        vbd = {u: _bd_rows(vnew[u].astype(BF16), bmask) for u in cur}
        for u in cur:
            refs[u[0]][4][rows(u[0], step), sl(u[2])] = ws[u][c:] + _dot(att[u], vbd[u])
        upd = {u: _dot_tn(kdec[u], vnew[u].astype(BF16)) for u in cur}
        for u in cur:
            s_ref[u[0] * NGRP + u[2]] = st[u] * egl[u] + jnp.where(bmask, upd[u], 0.0)


def _dn_chunk(qn, kn, vv, gb, *, b, s, nch):
    n = s // (DN_CHUNK * nch)
    t = b * s
    fwd = lambda bi, ci: (bi * n + ci, 0)
    bwd = lambda bi, ci: (bi * n + n - 1 - ci, 0)
    wide = lambda im: pl.BlockSpec((DN_CHUNK * nch, DN_W), im)
    small = lambda im: pl.BlockSpec((DN_CHUNK * nch, 128), im)
    return pl.pallas_call(
        functools.partial(_dn_chunk_kernel, nch=nch),
        grid=(b, n),
        in_specs=[wide(fwd), wide(fwd), wide(fwd), small(fwd), wide(bwd), wide(bwd), wide(bwd), small(bwd)],
        out_specs=[wide(fwd), wide(bwd)],
        out_shape=[jax.ShapeDtypeStruct((t, DN_W), F32)] * 2,
        scratch_shapes=[pltpu.VMEM((2 * NGRP, GRP, GRP), F32)],
        compiler_params=_cparams(("parallel", "arbitrary")),
        name="dn_chunk",
    )(qn, kn, vv, gb, qn, kn, vv, gb)


def _merge_kernel(x_ref, gates_ref, dz_ref, mq_ref, omla_ref, of_ref, ob_ref, mk_ref, mv_ref,
                  onorm_ref, gm_ref, wm_ref, wd_ref, wmem_ref, wout_ref, o_ref):
    o = of_ref[...] + ob_ref[...]
    ms = _group_sum(o * o, gm_ref[...]) * (1.0 / DN_DK)
    dz = dz_ref[...].astype(F32)
    o_dn = o * lax.rsqrt(ms + EPS) * onorm_ref[...] * (dz * _sigmoid(dz))

    mq = mq_ref[...]
    mk = mk_ref[0]
    mv = mv_ref[0]
    outs = []
    for h in range(MEM_HEADS):
        hs = slice(h * MEM_HD, (h + 1) * MEM_HD)
        sc = _dot_nt(mq[:, hs], mk[:, hs]) * (MEM_HD ** -0.5)
        p = jnp.exp(sc - jnp.max(sc, axis=-1, keepdims=True))
        p = p / jnp.sum(p, axis=-1, keepdims=True)
        outs.append(_dot(p.astype(BF16), mv[:, hs]))
    o_mem = jnp.concatenate(outs, axis=-1)

    gates = gates_ref[...].astype(F32)
    merged = (_sigmoid(gates[:, 0:D_MODEL]) * _dot(omla_ref[...], wm_ref[...])
              + _sigmoid(gates[:, D_MODEL:2 * D_MODEL]) * _dot(o_dn.astype(BF16), wd_ref[...])
              + _sigmoid(gates[:, 2 * D_MODEL:]) * _dot(o_mem.astype(BF16), wmem_ref[...]))
    o_ref[...] = x_ref[...] + _dot(merged.astype(BF16), wout_ref[...])


def _merge(x, proj, omla, o_f, o_b, memkv, onorm, gmat, wm, wd, wmem, wout, *, layer, b, s, tm):
    t = b * s
    nb = s // tm
    mt = memkv.shape[1]
    const = lambda shape: pl.BlockSpec(shape, lambda i: (0,) * len(shape))
    stacked = lambda shape: pl.BlockSpec((None,) + shape, lambda i: (layer, 0, 0))
    return pl.pallas_call(
        _merge_kernel,
        grid=(t // tm,),
        in_specs=[pl.BlockSpec((tm, D_MODEL), lambda i: (i, 0)),
                  pl.BlockSpec((tm, 3 * D_MODEL), lambda i: (i, COL_GATES // (3 * D_MODEL))),
                  pl.BlockSpec((tm, DN_W), lambda i: (i, COL_DZ // DN_W)),
                  pl.BlockSpec((tm, MEM_W), lambda i: (i, COL_MQ // MEM_W)),
                  pl.BlockSpec((tm, DN_W), lambda i: (i, 0)),
                  pl.BlockSpec((tm, DN_W), lambda i: (i, 0)),
                  pl.BlockSpec((tm, DN_W), lambda i: (i, 0)),
                  pl.BlockSpec((1, mt, MEM_W), lambda i: (i // nb, 0, 0)),
                  pl.BlockSpec((1, mt, MEM_W), lambda i: (i // nb, 0, 1)),
                  const((1, DN_W)), const((DN_W, DN_W)),
                  stacked((DN_W, D_MODEL)), stacked((DN_W, D_MODEL)), stacked((MEM_W, D_MODEL)),
                  stacked((D_MODEL, D_MODEL))],
        out_specs=pl.BlockSpec((tm, D_MODEL), lambda i: (i, 0)),
        out_shape=jax.ShapeDtypeStruct((t, D_MODEL), F32),
        compiler_params=_cparams(("parallel",)),
        name="merge",
    )(x, proj, proj, proj, omla, o_f, o_b, memkv, memkv, onorm, gmat, wm, wd, wmem, wout)


def _ffn_kernel(x_ref, nw_ref, wgu_ref, wd_ref, fw_ref, o_ref, *, nf, final_norm):
    x = x_ref[...]
    h = _rms(x, nw_ref[...]).astype(BF16)
    tf = D_FF // nf
    y = x
    for c in range(nf):
        gate = _dot(h, wgu_ref[:, c * tf:(c + 1) * tf])
        up = _dot(h, wgu_ref[:, D_FF + c * tf:D_FF + (c + 1) * tf])
        act = (gate * _sigmoid(gate) * up).astype(BF16)
        y = y + _dot(act, wd_ref[c * tf:(c + 1) * tf, :])
    if final_norm:
        y = _rms(y, fw_ref[...])
    o_ref[...] = y


def _ffn(x, nw, wgu, wd, fw, *, layer, tm, nf, final_norm):
    t = x.shape[0]
    resident = lambda shape: pl.BlockSpec((None,) + shape, lambda i: (layer, 0, 0), pipeline_mode=pl.Buffered(1))
    return pl.pallas_call(
        functools.partial(_ffn_kernel, nf=nf, final_norm=final_norm),
        grid=(t // tm,),
        in_specs=[pl.BlockSpec((tm, D_MODEL), lambda i: (i, 0)),
                  pl.BlockSpec((1, D_MODEL), lambda i: (0, 0)),
                  resident((D_MODEL, 2 * D_FF)),
                  resident((D_FF, D_MODEL)),
                  pl.BlockSpec((1, D_MODEL), lambda i: (0, 0))],
        out_specs=pl.BlockSpec((tm, D_MODEL), lambda i: (i, 0)),
        out_shape=jax.ShapeDtypeStruct((t, D_MODEL), F32),
        compiler_params=_cparams(("parallel",)),
        name="ffn",
    )(x, nw, wgu, wd, fw)


def _pick(n, pref):
    return pref if n % pref == 0 else n


def _rope_tables(positions):
    b, s = positions.shape
    half = MLA_ROPE // 2
    inv_freq = 1.0 / (ROPE_THETA ** (jnp.arange(0, MLA_ROPE, 2, dtype=F32) / MLA_ROPE))
    ang_t = positions.astype(F32)[:, None, :] * inv_freq[None, :, None]
    cos_t, sin_t = lax.optimization_barrier((jnp.cos(ang_t), jnp.sin(ang_t)))
    qscale = (MLA_QK ** -0.5) * LOG2E
    rows = lambda n, v: jnp.full((b, n, s), v, F32)
    cqt = jnp.concatenate([rows(MLA_NOPE, 1.0), cos_t, cos_t, rows(32, 0.0)], axis=1) * qscale
    sqt = jnp.concatenate([rows(MLA_NOPE, 0.0), sin_t, sin_t, rows(32, 0.0)], axis=1) * qscale
    cos = cos_t.transpose(0, 2, 1).reshape(b * s, half)
    sin = sin_t.transpose(0, 2, 1).reshape(b * s, half)
    z64 = jnp.zeros((b * s, MLA_NOPE), F32)
    z32 = jnp.zeros((b * s, 32), F32)
    ck = jnp.concatenate([z64, cos, cos, z32], axis=1)
    sk = jnp.concatenate([z64, sin, sin, z32], axis=1)
    return ck, sk, cqt, sqt


def _reorder_w_in(w_in):
    w_in = w_in.astype(BF16)
    z = lambda n: jnp.zeros(w_in.shape[:2] + (n,), BF16)
    kr = w_in[..., 512:544]
    return jnp.concatenate([
        w_in[..., 3136:6208], w_in[..., 544:2080], w_in[..., 2080:2592], w_in[..., 2624:3136],
        w_in[..., 0:256], w_in[..., 256:512],
        w_in[..., 2592:2624], z(32), kr, z(32),
        z(64), -kr[..., 16:32], kr[..., 0:16], z(32)], axis=-1)


def _mla_weights(w_uq, w_ukv):
    uq = w_uq.reshape(MLA_Q_LORA, MLA_HEADS, MLA_QK)
    zq = lambda n: jnp.zeros((MLA_Q_LORA, MLA_HEADS, n), w_uq.dtype)
    x1, x2 = uq[..., 64:80], uq[..., 80:96]
    wq1 = jnp.concatenate([uq[..., :64], x1, x2, zq(32)], axis=-1).reshape(MLA_Q_LORA, 1024)
    wq2 = jnp.concatenate([zq(64), -x2, x1, zq(32)], axis=-1).reshape(MLA_Q_LORA, 1024)
    ukv = w_ukv.reshape(MLA_KV_LORA, MLA_HEADS, MLA_NOPE + MLA_V)
    wk = jnp.concatenate([ukv[..., :64], jnp.zeros((MLA_KV_LORA, MLA_HEADS, 64), w_ukv.dtype)],
                         axis=-1).reshape(MLA_KV_LORA, 1024)
    wv = ukv[..., 64:].reshape(MLA_KV_LORA, MLA_HEADS * MLA_V)
    return wq1.T.astype(BF16), wq2.T.astype(BF16), wk.astype(BF16), wv.T.astype(BF16)


def kernel(x, mem, positions, norm_mix, w_in, mla_q_norm, mla_w_uq, mla_kv_norm, mla_w_ukv, dn_conv, dn_a_log, dn_dt_bias, dn_out_norm, mem_norm, mem_w_kv, w_branch_mla, w_branch_dn, w_branch_mem, w_out, norm_ffn, ffn_w_gate_up, ffn_w_down, final_norm):
    b, s, d = x.shape
    t = b * s
    depth = w_in.shape[0]
    mt = mem.shape[1]
    ck, sk, cqt, sqt = _rope_tables(positions)
    gi = jnp.arange(DN_W) // DN_DK
    gmat = (gi[:, None] == gi[None, :]).astype(BF16)
    xf = x.reshape(t, d)
    memf = mem.reshape(b * mt, d)
    tm = _pick(s, 512)
    w_re = _reorder_w_in(w_in)
    memw, wbm, wbd, wbmem, wout = (a.astype(BF16) for a in (mem_w_kv, w_branch_mla, w_branch_dn, w_branch_mem, w_out))
    wgu, wdn = ffn_w_gate_up.astype(BF16), ffn_w_down.astype(BF16)
    for l in range(depth):
        wq1t, wq2t, wk, wvt = _mla_weights(mla_w_uq[l], mla_w_ukv[l])
        proj, small = _norm_matmul(xf, norm_mix[l][None], w_re, layer=l, tm=_pick(t, 2048), tn=1280,
                                   tail=IN_COLS_P - COL_SMALL)

        qt, kk, vt = _mla_prep(proj, small, ck, sk, cqt, sqt, mla_q_norm[l][None], mla_kv_norm[l][None],
                               wq1t, wq2t, wk, wvt, b=b, s=s, tm=tm)
        o_mla = _flash(qt, kk, vt, b=b, s=s, tq=_pick(s, 512), tk=_pick(s, 256), unroll=min(32, s // _pick(s, 256)))

        conv_w = jnp.concatenate([dn_conv[l], jnp.zeros((8 - DN_CONV, 3 * DN_W), F32)], axis=0)
        pad16 = lambda v: jnp.concatenate([v.reshape(-1), jnp.zeros((128 - 2 * DN_HEADS,), F32)])
        par = jnp.concatenate([pad16(dn_a_log[l])[None], pad16(dn_dt_bias[l])[None],
                               jnp.zeros((6, 128), F32)], axis=0)
        qn, kn, vv, gb = _dn_prep(proj, small, conv_w, par, gmat, b=b, s=s, tm=tm)
        o_f, o_b = _dn_chunk(qn, kn, vv, gb, b=b, s=s, nch=4)
        n_chunks = s // DN_CHUNK
        perm = lambda a: a.reshape(b, n_chunks, DN_CHUNK, DN_W).swapaxes(1, 2).reshape(t, DN_W)
        o_f, o_b = perm(o_f), perm(o_b)

        memkv = _norm_matmul(memf, mem_norm[l][None], memw, layer=l,
                             tm=b * mt, tn=512)[0].reshape(b, mt, 2 * MEM_W)
        onorm = jnp.tile(dn_out_norm[l], DN_HEADS)[None]
        xf = _merge(xf, proj, o_mla, o_f, o_b, memkv, onorm, gmat, wbm, wbd, wbmem, wout,
                    layer=l, b=b, s=s, tm=tm)
        xf = _ffn(xf, norm_ffn[l][None], wgu, wdn, final_norm[None],
                  layer=l, tm=_pick(s, 1024), nf=11, final_norm=(l == depth - 1))
    return xf.reshape(b, s, d)
```

```python
import functools
from typing import NamedTuple

import jax
import jax.numpy as jnp
from jax import lax
from jax.experimental import pallas as pl
from jax.experimental.pallas import tpu as pltpu

F32 = jnp.float32
BF16 = jnp.bfloat16

D_MODEL = 1024
EPS = 1e-6
MLA_HEADS = 8
MLA_Q_LORA = 256
MLA_KV_LORA = 256
MLA_NOPE = 64
MLA_ROPE = 32
MLA_V = 64
MLA_QK = MLA_NOPE + MLA_ROPE
ROPE_THETA = 10000.0
DN_HEADS = 8
DN_DK = 64
DN_W = DN_HEADS * DN_DK
DN_CONV = 5
DN_CHUNK = 64
MEM_HEADS = 4
MEM_HD = 128
MEM_W = MEM_HEADS * MEM_HD
D_FF = 2816

COL_GATES = 0
COL_DQKV = 3072
COL_DZ = 4608
COL_MQ = 5120
COL_CQ = 5632
COL_CKV = 5888
COL_SMALL = 6144
IN_COLS_P = 6400

HEAD_SLAB = 128
VT_ROWS = 80
SOFTMAX_STRIP = 256
LOG2E = 1.4426950408889634
VMEM_LIMIT = 48 * 1024 * 1024


def _cparams(sem):
    return pltpu.CompilerParams(dimension_semantics=sem, vmem_limit_bytes=VMEM_LIMIT)


def _sigmoid(x):
    return 0.5 * jnp.tanh(0.5 * x) + 0.5


def _dot(a, b):
    return jnp.dot(a, b, preferred_element_type=F32)


def _dot_nt(a, b):
    return lax.dot_general(a, b, (((1,), (1,)), ((), ())), preferred_element_type=F32)


def _dot_tn(a, b):
    return lax.dot_general(a, b, (((0,), (0,)), ((), ())), preferred_element_type=F32)


def _split3(x):
    x1 = x.astype(BF16)
    r1 = x - x1.astype(F32)
    x2 = r1.astype(BF16)
    x3 = (r1 - x2.astype(F32)).astype(BF16)
    return x1, x2, x3


def _rms(x, w):
    return x * lax.rsqrt(jnp.mean(x * x, axis=-1, keepdims=True) + EPS) * w


def _norm_matmul_kernel(x_ref, g_ref, w_ref, o_ref, *rest, tail):
    h_ref = rest[-1]

    @pl.when(pl.program_id(1) == 0)
    def _():
        h_ref[...] = _rms(x_ref[...], g_ref[...]).astype(BF16)

    acc = _dot(h_ref[...], w_ref[...])
    o_ref[...] = acc.astype(o_ref.dtype)
    if tail:
        @pl.when(pl.program_id(1) == pl.num_programs(1) - 1)
        def _():
            rest[0][...] = acc[:, acc.shape[1] - tail:]


def _norm_matmul(x, g, w, *, layer, tm, tn, tail=0):
    t, d = x.shape
    n = w.shape[2]
    out_specs = [pl.BlockSpec((tm, tn), lambda i, j: (i, j))]
    out_shape = [jax.ShapeDtypeStruct((t, n), BF16)]
    if tail:
        out_specs.append(pl.BlockSpec((tm, tail), lambda i, j: (i, 0)))
        out_shape.append(jax.ShapeDtypeStruct((t, tail), F32))
    return pl.pallas_call(
        functools.partial(_norm_matmul_kernel, tail=tail),
        grid=(t // tm, n // tn),
        in_specs=[pl.BlockSpec((tm, d), lambda i, j: (i, 0)),
                  pl.BlockSpec((1, d), lambda i, j: (0, 0)),
                  pl.BlockSpec((None, d, tn), lambda i, j: (layer, 0, j))],
        out_specs=out_specs,
        out_shape=out_shape,
        scratch_shapes=[pltpu.VMEM((tm, d), BF16)],
        compiler_params=_cparams(("parallel", "arbitrary")),
        name="norm_matmul",
    )(x, g, w)


def _mla_prep_kernel(cq_ref, ckv_ref, sm_ref, ck_ref, sk_ref, cqt_ref, sqt_ref, qn_ref, kvn_ref,
                     wq1t_ref, wq2t_ref, wk_ref, wvt_ref, qt_out, k_out, vt_out):
    qn = _rms(cq_ref[...].astype(F32), qn_ref[...]).astype(BF16)
    q1t = _dot_nt(wq1t_ref[...], qn)
    q2t = _dot_nt(wq2t_ref[...], qn)
    ct = cqt_ref[0]
    st = sqt_ref[0]
    for h in range(MLA_HEADS):
        hs = slice(h * HEAD_SLAB, (h + 1) * HEAD_SLAB)
        qt_out[0, hs, :] = (q1t[hs] * ct + q2t[hs] * st).astype(BF16)

    kvn = _rms(ckv_ref[...].astype(F32), kvn_ref[...]).astype(BF16)
    kk = _dot(kvn, wk_ref[...])
    sm = sm_ref[...]
    kr = sm[:, :128] * ck_ref[...] + sm[:, 128:] * sk_ref[...]
    for h in range(MLA_HEADS):
        hs = slice(h * HEAD_SLAB, (h + 1) * HEAD_SLAB)
        k_out[:, hs] = (kk[:, hs] + kr).astype(BF16)

    vt = _dot_nt(wvt_ref[...], kvn)
    tm = vt.shape[1]
    ones = jnp.ones((VT_ROWS - MLA_V, tm), BF16)
    for h in range(MLA_HEADS):
        vt_out[0, h * VT_ROWS:h * VT_ROWS + MLA_V, :] = vt[h * MLA_V:(h + 1) * MLA_V].astype(BF16)
        vt_out[0, h * VT_ROWS + MLA_V:(h + 1) * VT_ROWS, :] = ones


def _mla_prep(proj, small, ck, sk, cqt, sqt, qnorm, kvnorm, wq1t, wq2t, wk, wvt, *, b, s, tm):
    nb = s // tm
    t = b * s
    row = lambda bi, i: bi * nb + i
    const = lambda shape: pl.BlockSpec(shape, lambda bi, i: (0,) * len(shape))
    return pl.pallas_call(
        _mla_prep_kernel,
        grid=(b, nb),
        in_specs=[pl.BlockSpec((tm, 256), lambda bi, i: (row(bi, i), COL_CQ // 256)),
                  pl.BlockSpec((tm, 256), lambda bi, i: (row(bi, i), COL_CKV // 256)),
                  pl.BlockSpec((tm, 256), lambda bi, i: (row(bi, i), 0)),
                  pl.BlockSpec((tm, 128), lambda bi, i: (row(bi, i), 0)),
                  pl.BlockSpec((tm, 128), lambda bi, i: (row(bi, i), 0)),
                  pl.BlockSpec((1, 128, tm), lambda bi, i: (bi, 0, i)),
                  pl.BlockSpec((1, 128, tm), lambda bi, i: (bi, 0, i)),
                  const((1, 256)), const((1, 256)),
                  const((1024, 256)), const((1024, 256)), const((256, 1024)), const((512, 256))],
        out_specs=[pl.BlockSpec((1, 1024, tm), lambda bi, i: (bi, 0, i)),
                   pl.BlockSpec((tm, 1024), lambda bi, i: (row(bi, i), 0)),
                   pl.BlockSpec((1, MLA_HEADS * VT_ROWS, tm), lambda bi, i: (bi, 0, i))],
        out_shape=[jax.ShapeDtypeStruct((b, 1024, s), BF16),
                   jax.ShapeDtypeStruct((t, 1024), BF16),
                   jax.ShapeDtypeStruct((b, MLA_HEADS * VT_ROWS, s), BF16)],
        compiler_params=_cparams(("parallel", "parallel")),
        name="mla_prep",
    )(proj, proj, small, ck, sk, cqt, sqt, qnorm, kvnorm, wq1t, wq2t, wk, wvt)


def _chunk_start(j, tk):
    return j * tk if isinstance(j, int) else pl.multiple_of(j * tk, tk)


def _flash_kernel(qt_ref, k_ref, vt_ref, o_ref, m_ref, acc_ref, s_buf, p_buf, a_buf, *, tq, tk, nq, nk, unroll):
    total = nq * nk
    lognk = nk.bit_length() - 1
    m_ref[...] = jnp.full(m_ref.shape, -jnp.inf, F32)
    acc_ref[...] = jnp.zeros(acc_ref.shape, F32)

    def split(n):
        if isinstance(n, int):
            return n // nk, n % nk
        return n >> lognk, n & (nk - 1)

    def scores(n, par):
        qi, j = split(n)
        q0 = _chunk_start(qi, tq)
        r0 = _chunk_start(j, tk)
        for h in range(2):
            hs = slice(h * HEAD_SLAB, (h + 1) * HEAD_SLAB)
            s_buf[par, h] = _dot(k_ref[pl.ds(r0, tk), hs],
                                 qt_ref[0, hs, pl.ds(q0, tq)]).astype(s_buf.dtype)

    def softmax(n, par):
        _, j = split(n)
        for h in range(2):
            for c0 in range(0, tq, SOFTMAX_STRIP):
                cs = slice(c0, c0 + SOFTMAX_STRIP)
                st = s_buf[par, h, :, cs]
                m_old = jnp.where(j == 0, -jnp.inf, m_ref[h, :, cs])
                m_new = jnp.maximum(m_old, jnp.max(st, axis=0, keepdims=True).astype(F32))
                a_buf[par, h, :, cs] = jnp.exp2(m_old - m_new)
                p_buf[par, h, :, cs] = jnp.exp2(st - m_new.astype(st.dtype)).astype(BF16)
                m_ref[h, :, cs] = m_new

    def pv(n, par):
        _, j = split(n)
        r0 = _chunk_start(j, tk)
        for h in range(2):
            vc = vt_ref[0, h * VT_ROWS:(h + 1) * VT_ROWS, pl.ds(r0, tk)]
            acc_ref[h] = a_buf[par, h] * acc_ref[h] + _dot(vc, p_buf[par, h])

    def finalize(qi):
        outs = []
        for h in range(2):
            a = acc_ref[h]
            outs.append(a[0:MLA_V] / a[MLA_V:MLA_V + 1])
        ot = jnp.concatenate(outs, axis=0)
        o_ref[pl.ds(_chunk_start(qi, tq), tq), :] = ot.T.astype(o_ref.dtype)

    scores(0, 0)
    scores(1, 1)
    softmax(0, 0)
    trips = (total - 2) // unroll
    assert trips == 0 or (nk % unroll == 0 and nk == 1 << lognk and unroll % 2 == 0)

    def body(i, carry):
        for u in range(unroll):
            n = unroll * i + u
            pv(n, u % 2)
            softmax(n + 1, (u + 1) % 2)
            scores(n + 2, u % 2)
        n_last = unroll * i + unroll - 1

        @pl.when((n_last & (nk - 1)) == nk - 1)
        def _():
            finalize(n_last >> lognk)
        return carry

    lax.fori_loop(0, trips, body, 0)
    for n in range(trips * unroll, total):
        pv(n, n % 2)
        if n + 1 < total:
            softmax(n + 1, (n + 1) % 2)
        if n + 2 < total:
            scores(n + 2, n % 2)
        if n % nk == nk - 1:
            finalize(n // nk)


def _flash(qt, k, vt, *, b, s, tq, tk, unroll):
    nq = s // tq
    nk = s // tk
    hp = MLA_HEADS // 2
    return pl.pallas_call(
        functools.partial(_flash_kernel, tq=tq, tk=tk, nq=nq, nk=nk, unroll=unroll),
        grid=(b, hp),
        in_specs=[pl.BlockSpec((1, 2 * HEAD_SLAB, s), lambda bi, p: (bi, p, 0)),
                  pl.BlockSpec((s, 2 * HEAD_SLAB), lambda bi, p: (bi, p)),
                  pl.BlockSpec((1, 2 * VT_ROWS, s), lambda bi, p: (bi, p, 0))],
        out_specs=pl.BlockSpec((s, 2 * MLA_V), lambda bi, p: (bi, p)),
        out_shape=jax.ShapeDtypeStruct((b * s, MLA_HEADS * MLA_V), BF16),
        scratch_shapes=[pltpu.VMEM((2, 1, tq), F32), pltpu.VMEM((2, VT_ROWS, tq), F32),
                        pltpu.VMEM((2, 2, tk, tq), BF16), pltpu.VMEM((2, 2, tk, tq), BF16),
                        pltpu.VMEM((2, 2, 1, tq), F32)],
        compiler_params=_cparams(("parallel", "arbitrary")),
        name="mla_flash",
    )(qt, k, vt)


def _group_sum(z, gmat):
    z1 = z.astype(BF16)
    z2 = (z - z1.astype(F32)).astype(BF16)
    return _dot(z1, gmat) + _dot(z2, gmat)


HALO = 16


def _dn_prep_kernel(x_ref, xp_ref, xn_ref, sm_ref, cw_ref, par_ref, gm_ref,
                    q_out, k_out, v_out, gb_out, pad_ref, *, nb, tm):
    i = pl.program_id(0)
    first = (i % nb) == 0
    last = (i % nb) == nb - 1
    pad_ref[0:HALO, :] = jnp.where(first, 0.0, xp_ref[...].astype(F32))
    pad_ref[HALO:HALO + tm, :] = x_ref[...].astype(F32)
    pad_ref[HALO + tm:2 * HALO + tm, :] = jnp.where(last, 0.0, xn_ref[...].astype(F32))
    cw = cw_ref[...]
    off = HALO - DN_CONV // 2
    y = cw[0:1] * pad_ref[pl.ds(off, tm), :]
    for j in range(1, DN_CONV):
        y = y + cw[j:j + 1] * pad_ref[pl.ds(off + j, tm), :]
    y = y * _sigmoid(y)
    gm = gm_ref[...]
    q = y[:, 0:DN_W]
    k = y[:, DN_W:2 * DN_W]
    q_out[...] = q * lax.rsqrt(_group_sum(q * q, gm) + EPS) * (DN_DK ** -0.5)
    k_out[...] = k * lax.rsqrt(_group_sum(k * k, gm) + EPS)
    v_out[...] = y[:, 2 * DN_W:]

    sm = sm_ref[...]
    par = par_ref[...]
    z = sm + par[1:2]
    softplus = jnp.maximum(z, 0.0) + jnp.log1p(jnp.exp(-jnp.abs(z)))
    g = -jnp.exp(par[0:1]) * softplus
    lane = lax.broadcasted_iota(jnp.int32, sm.shape, 1)
    gb_out[...] = jnp.where(lane < 16, g, _sigmoid(sm))


def _dn_prep(proj, small, conv_w, par, gmat, *, b, s, tm):
    t = b * s
    nb = s // tm
    w3 = 3 * DN_W
    cb = COL_DQKV // w3
    hb = tm // HALO
    last_hb = t // HALO - 1
    return pl.pallas_call(
        functools.partial(_dn_prep_kernel, nb=nb, tm=tm),
        grid=(t // tm,),
        in_specs=[pl.BlockSpec((tm, w3), lambda i: (i, cb)),
                  pl.BlockSpec((HALO, w3), lambda i: (jnp.maximum(i * hb - 1, 0), cb)),
                  pl.BlockSpec((HALO, w3), lambda i: (jnp.minimum((i + 1) * hb, last_hb), cb)),
                  pl.BlockSpec((tm, 128), lambda i: (i, 0)),
                  pl.BlockSpec((8, w3), lambda i: (0, 0)),
                  pl.BlockSpec((8, 128), lambda i: (0, 0)),
                  pl.BlockSpec((DN_W, DN_W), lambda i: (0, 0))],
        out_specs=[pl.BlockSpec((tm, DN_W), lambda i: (i, 0)),
                   pl.BlockSpec((tm, DN_W), lambda i: (i, 0)),
                   pl.BlockSpec((tm, DN_W), lambda i: (i, 0)),
                   pl.BlockSpec((tm, 128), lambda i: (i, 0))],
        out_shape=[jax.ShapeDtypeStruct((t, DN_W), F32)] * 3 + [jax.ShapeDtypeStruct((t, 128), F32)],
        scratch_shapes=[pltpu.VMEM((tm + 2 * HALO, w3), F32)],
        compiler_params=_cparams(("parallel",)),
        name="dn_prep",
    )(proj, proj, proj, small, conv_w, par, gmat)


GRP = 256
NGRP = DN_W // GRP


def _bd_rows(x_bf, bmask):
    return jnp.where(bmask, jnp.concatenate([x_bf] * (GRP // DN_CHUNK), axis=0), jnp.zeros((), BF16))


def _dn_chunk_kernel(qf_ref, kf_ref, vf_ref, gf_ref, qb_ref, kb_ref, vb_ref, gb_ref,
                     of_ref, ob_ref, s_ref, *, nch):
    c = DN_CHUNK

    @pl.when(pl.program_id(1) == 0)
    def _():
        s_ref[...] = jnp.zeros(s_ref.shape, F32)

    ri = lax.broadcasted_iota(jnp.int32, (c, GRP), 0)
    cj = lax.broadcasted_iota(jnp.int32, (c, GRP), 1) & (c - 1)
    br = lax.broadcasted_iota(jnp.int32, (GRP, GRP), 0) >> 6
    bc = lax.broadcasted_iota(jnp.int32, (GRP, GRP), 1) >> 6
    bmask = br == bc
    ident = (ri == cj).astype(F32)
    incl = (ri >= cj, ri <= cj)
    strict = (ri > cj, ri < cj)

    def pmm(a, bmat):
        return _dot(a.astype(BF16), _bd_rows(bmat.astype(BF16), bmask))

    def pmm_nt(a, bmat):
        return _dot_nt(a.astype(BF16), _bd_rows(bmat.astype(BF16), bmask))

    er = lax.broadcasted_iota(jnp.int32, (128, 2 * DN_W), 0)
    ec = lax.broadcasted_iota(jnp.int32, (128, 2 * DN_W), 1) >> 6
    tr = lax.broadcasted_iota(jnp.int32, (c, c), 0)
    tc = lax.broadcasted_iota(jnp.int32, (c, c), 1)
    mr = lax.broadcasted_iota(jnp.int32, (c, DN_W), 0)
    mj = lax.broadcasted_iota(jnp.int32, (c, DN_W), 1) & (c - 1)
    ones_cc = jnp.ones((c, c), BF16)
    tri = ((tc <= tr).astype(BF16), (tc >= tr).astype(BF16))
    mask_t = (mr <= mj, mr >= mj)
    refs = ((qf_ref, kf_ref, vf_ref, gf_ref, of_ref), (qb_ref, kb_ref, vb_ref, gb_ref, ob_ref))

    def rows(d, step):
        i = step if d == 0 else nch - 1 - step
        return slice(i * c, (i + 1) * c)

    def sum3(lhs, rhs3):
        return _dot(lhs, rhs3[0]) + _dot(lhs, rhs3[1]) + _dot(lhs, rhs3[2])

    ds = [(d, step) for step in range(nch) for d in (0, 1)]
    e_gb = [(er == jnp.where(ec < DN_HEADS, d * DN_HEADS + ec, 8 + d * DN_HEADS + ec)).astype(BF16)
            for d in (0, 1)]
    g3 = {u: jnp.concatenate(_split3(refs[u[0]][3][rows(*u), :]), axis=0) for u in ds}
    gbx = {u: _dot(g3[u], e_gb[u[0]]) for u in ds}
    gbx = {u: gbx[u][0:c] + gbx[u][c:2 * c] + gbx[u][2 * c:3 * c] for u in ds}
    gexp = {u: gbx[u][:, :DN_W] for u in ds}
    bexp = {u: gbx[u][:, DN_W:] for u in ds}
    ge3 = {u: _split3(gexp[u]) for u in ds}
    gm3 = {u: _split3(jnp.where(mask_t[u[0]], gexp[u], 0.0)) for u in ds}
    gcrow = {u: sum3(tri[u[0]], ge3[u]) for u in ds}
    gccol = {u: sum3(ones_cc, gm3[u]) for u in ds}

    units = [(d, step, g) for step in range(nch) for d in (0, 1) for g in range(NGRP)]
    sl = lambda g: slice(g * GRP, (g + 1) * GRP)
    kg = {u: refs[u[0]][1][rows(u[0], u[1]), sl(u[2])] for u in units}
    qg = {u: refs[u[0]][0][rows(u[0], u[1]), sl(u[2])] for u in units}
    vg = {u: refs[u[0]][2][rows(u[0], u[1]), sl(u[2])] for u in units}
    bg = {u: bexp[u[:2]][:, sl(u[2])] for u in units}
    gc = {u: gcrow[u[:2]][:, sl(u[2])] for u in units}
    dec = {u: jnp.exp(jnp.where(incl[u[0]], gc[u] - gccol[u[:2]][:, sl(u[2])], -jnp.inf)) for u in units}
    egc = {u: jnp.exp(gc[u]) for u in units}
    kq = {u: pmm_nt(jnp.concatenate([kg[u], qg[u]], axis=0), kg[u]) for u in units}
    kk = {u: kq[u][0:c] for u in units}
    qk = {u: kq[u][c:] for u in units}
    lm = {u: jnp.where(strict[u[0]], kk[u] * bg[u] * dec[u], 0.0) for u in units}

    dd = {u: jnp.where((ri >> 3) == (cj >> 3), lm[u], 0.0) for u in units}
    d2 = {u: pmm(dd[u], dd[u]) for u in units}
    d4 = {u: pmm(d2[u], d2[u]) for u in units}
    x = {u: pmm(ident - dd[u], ident + d2[u]) for u in units}
    x = {u: pmm(x[u], ident + d4[u]) for u in units}
    for sh in (3, 4, 5):
        off = ((ri >> (sh + 1)) == (cj >> (sh + 1))) & ((ri >> sh) != (cj >> sh))
        xm = {u: pmm(x[u], jnp.where(off, lm[u], 0.0)) for u in units}
        x = {u: x[u] - pmm(xm[u], x[u]) for u in units}
    uu = {u: pmm(x[u], vg[u] * bg[u]) for u in units}
    ww = {u: pmm(x[u], kg[u] * bg[u] * egc[u]).astype(BF16) for u in units}
    att = {u: jnp.where(incl[u[0]], qk[u] * dec[u], 0.0).astype(BF16) for u in units}
    wq = {u: jnp.concatenate([ww[u], (qg[u] * egc[u]).astype(BF16)], axis=0) for u in units}
    gl = {u: (gc[u][c - 1:c] if u[0] == 0 else gc[u][0:1]) for u in units}
    kdec = {u: (kg[u] * jnp.exp(gl[u] - gc[u])).astype(BF16) for u in units}
    egl = {u: jnp.exp(gl[u]) for u in units}

    for step in range(nch):
        cur = [(d, step, g) for d in (0, 1) for g in range(NGRP)]
        st = {u: s_ref[u[0] * NGRP + u[2]] for u in cur}
        ws = {u: _dot(wq[u], st[u].astype(BF16)) for u in cur}
        vnew = {u: uu[u] - ws[u][0:c] for u in cur}
        vbd = {u: _bd_rows(vnew[u].astype(BF16), bmask) for u in cur}
        for u in cur:
            refs[u[0]][4][rows(u[0], step), sl(u[2])] = ws[u][c:] + _dot(att[u], vbd[u])
        upd = {u: _dot_tn(kdec[u], vnew[u].astype(BF16)) for u in cur}
        for u in cur:
            s_ref[u[0] * NGRP + u[2]] = st[u] * egl[u] + jnp.where(bmask, upd[u], 0.0)


def _dn_chunk(qn, kn, vv, gb, *, b, s, nch):
    n = s // (DN_CHUNK * nch)
    t = b * s
    fwd = lambda bi, ci: (bi * n + ci, 0)
    bwd = lambda bi, ci: (bi * n + n - 1 - ci, 0)
    wide = lambda im: pl.BlockSpec((DN_CHUNK * nch, DN_W), im)
    small = lambda im: pl.BlockSpec((DN_CHUNK * nch, 128), im)
    return pl.pallas_call(
        functools.partial(_dn_chunk_kernel, nch=nch),
        grid=(b, n),
        in_specs=[wide(fwd), wide(fwd), wide(fwd), small(fwd), wide(bwd), wide(bwd), wide(bwd), small(bwd)],
        out_specs=[wide(fwd), wide(bwd)],
        out_shape=[jax.ShapeDtypeStruct((t, DN_W), F32)] * 2,
        scratch_shapes=[pltpu.VMEM((2 * NGRP, GRP, GRP), F32)],
        compiler_params=_cparams(("parallel", "arbitrary")),
        name="dn_chunk",
    )(qn, kn, vv, gb, qn, kn, vv, gb)


def _merge_kernel(x_ref, gates_ref, dz_ref, mq_ref, omla_ref, of_ref, ob_ref, mk_ref, mv_ref,
                  onorm_ref, gm_ref, wm_ref, wd_ref, wmem_ref, wout_ref, o_ref):
    o = of_ref[...] + ob_ref[...]
    ms = _group_sum(o * o, gm_ref[...]) * (1.0 / DN_DK)
    dz = dz_ref[...].astype(F32)
    o_dn = o * lax.rsqrt(ms + EPS) * onorm_ref[...] * (dz * _sigmoid(dz))

    mq = mq_ref[...]
    mk = mk_ref[0]
    mv = mv_ref[0]
    outs = []
    for h in range(MEM_HEADS):
        hs = slice(h * MEM_HD, (h + 1) * MEM_HD)
        sc = _dot_nt(mq[:, hs], mk[:, hs]) * (MEM_HD ** -0.5)
        p = jnp.exp(sc - jnp.max(sc, axis=-1, keepdims=True))
        p = p / jnp.sum(p, axis=-1, keepdims=True)
        outs.append(_dot(p.astype(BF16), mv[:, hs]))
    o_mem = jnp.concatenate(outs, axis=-1)

    gates = gates_ref[...].astype(F32)
    merged = (_sigmoid(gates[:, 0:D_MODEL]) * _dot(omla_ref[...], wm_ref[...])
              + _sigmoid(gates[:, D_MODEL:2 * D_MODEL]) * _dot(o_dn.astype(BF16), wd_ref[...])
              + _sigmoid(gates[:, 2 * D_MODEL:]) * _dot(o_mem.astype(BF16), wmem_ref[...]))
    o_ref[...] = x_ref[...] + _dot(merged.astype(BF16), wout_ref[...])


def _merge(x, proj, omla, o_f, o_b, memkv, onorm, gmat, wm, wd, wmem, wout, *, layer, b, s, tm):
    t = b * s
    nb = s // tm
    mt = memkv.shape[1]
    const = lambda shape: pl.BlockSpec(shape, lambda i: (0,) * len(shape))
    stacked = lambda shape: pl.BlockSpec((None,) + shape, lambda i: (layer, 0, 0))
    return pl.pallas_call(
        _merge_kernel,
        grid=(t // tm,),
        in_specs=[pl.BlockSpec((tm, D_MODEL), lambda i: (i, 0)),
                  pl.BlockSpec((tm, 3 * D_MODEL), lambda i: (i, COL_GATES // (3 * D_MODEL))),
                  pl.BlockSpec((tm, DN_W), lambda i: (i, COL_DZ // DN_W)),
                  pl.BlockSpec((tm, MEM_W), lambda i: (i, COL_MQ // MEM_W)),
                  pl.BlockSpec((tm, DN_W), lambda i: (i, 0)),
                  pl.BlockSpec((tm, DN_W), lambda i: (i, 0)),
                  pl.BlockSpec((tm, DN_W), lambda i: (i, 0)),
                  pl.BlockSpec((1, mt, MEM_W), lambda i: (i // nb, 0, 0)),
                  pl.BlockSpec((1, mt, MEM_W), lambda i: (i // nb, 0, 1)),
                  const((1, DN_W)), const((DN_W, DN_W)),
                  stacked((DN_W, D_MODEL)), stacked((DN_W, D_MODEL)), stacked((MEM_W, D_MODEL)),
                  stacked((D_MODEL, D_MODEL))],
        out_specs=pl.BlockSpec((tm, D_MODEL), lambda i: (i, 0)),
        out_shape=jax.ShapeDtypeStruct((t, D_MODEL), F32),
        compiler_params=_cparams(("parallel",)),
        name="merge",
    )(x, proj, proj, proj, omla, o_f, o_b, memkv, memkv, onorm, gmat, wm, wd, wmem, wout)


def _ffn_kernel(x_ref, nw_ref, wgu_ref, wd_ref, fw_ref, o_ref, *, nf, final_norm):
    x = x_ref[...]
    h = _rms(x, nw_ref[...]).astype(BF16)
    tf = D_FF // nf
    y = x
    for c in range(nf):
        gate = _dot(h, wgu_ref[:, c * tf:(c + 1) * tf])
        up = _dot(h, wgu_ref[:, D_FF + c * tf:D_FF + (c + 1) * tf])
        act = (gate * _sigmoid(gate) * up).astype(BF16)
        y = y + _dot(act, wd_ref[c * tf:(c + 1) * tf, :])
    if final_norm:
        y = _rms(y, fw_ref[...])
    o_ref[...] = y


def _ffn(x, nw, wgu, wd, fw, *, layer, tm, nf, final_norm):
    t = x.shape[0]
    resident = lambda shape: pl.BlockSpec((None,) + shape, lambda i: (layer, 0, 0), pipeline_mode=pl.Buffered(1))
    return pl.pallas_call(
        functools.partial(_ffn_kernel, nf=nf, final_norm=final_norm),
        grid=(t // tm,),
        in_specs=[pl.BlockSpec((tm, D_MODEL), lambda i: (i, 0)),
                  pl.BlockSpec((1, D_MODEL), lambda i: (0, 0)),
                  resident((D_MODEL, 2 * D_FF)),
                  resident((D_FF, D_MODEL)),
                  pl.BlockSpec((1, D_MODEL), lambda i: (0, 0))],
        out_specs=pl.BlockSpec((tm, D_MODEL), lambda i: (i, 0)),
        out_shape=jax.ShapeDtypeStruct((t, D_MODEL), F32),
        compiler_params=_cparams(("parallel",)),
        name="ffn",
    )(x, nw, wgu, wd, fw)


def _pick(n, pref):
    return pref if n % pref == 0 else n


class _Tiles(NamedTuple):
    rows: int
    proj_rows: int
    proj_cols: int
    ffn_rows: int
    ffn_slices: int
    flash_q: int
    flash_kv: int
    flash_unroll: int
    dn_chunks: int


def _tiles(b, s):
    kv = _pick(s, 256)
    return _Tiles(rows=_pick(s, 512), proj_rows=_pick(b * s, 2048), proj_cols=1280,
                  ffn_rows=_pick(s, 1024), ffn_slices=D_FF // 256,
                  flash_q=_pick(s, 512), flash_kv=kv, flash_unroll=min(32, s // kv), dn_chunks=4)


def _rope_tables(positions):
    b, s = positions.shape
    half = MLA_ROPE // 2
    inv_freq = 1.0 / (ROPE_THETA ** (jnp.arange(0, MLA_ROPE, 2, dtype=F32) / MLA_ROPE))
    ang_t = positions.astype(F32)[:, None, :] * inv_freq[None, :, None]
    cos_t, sin_t = lax.optimization_barrier((jnp.cos(ang_t), jnp.sin(ang_t)))
    qscale = (MLA_QK ** -0.5) * LOG2E
    rows = lambda n, v: jnp.full((b, n, s), v, F32)
    cqt = jnp.concatenate([rows(MLA_NOPE, 1.0), cos_t, cos_t, rows(32, 0.0)], axis=1) * qscale
    sqt = jnp.concatenate([rows(MLA_NOPE, 0.0), sin_t, sin_t, rows(32, 0.0)], axis=1) * qscale
    cos = cos_t.transpose(0, 2, 1).reshape(b * s, half)
    sin = sin_t.transpose(0, 2, 1).reshape(b * s, half)
    z64 = jnp.zeros((b * s, MLA_NOPE), F32)
    z32 = jnp.zeros((b * s, 32), F32)
    ck = jnp.concatenate([z64, cos, cos, z32], axis=1)
    sk = jnp.concatenate([z64, sin, sin, z32], axis=1)
    return ck, sk, cqt, sqt


def _reorder_w_in(w_in):
    w_in = w_in.astype(BF16)
    z = lambda n: jnp.zeros(w_in.shape[:2] + (n,), BF16)
    kr = w_in[..., 512:544]
    return jnp.concatenate([
        w_in[..., 3136:6208], w_in[..., 544:2080], w_in[..., 2080:2592], w_in[..., 2624:3136],
        w_in[..., 0:256], w_in[..., 256:512],
        w_in[..., 2592:2624], z(32), kr, z(32),
        z(64), -kr[..., 16:32], kr[..., 0:16], z(32)], axis=-1)


def _mla_weights(w_uq, w_ukv):
    uq = w_uq.reshape(MLA_Q_LORA, MLA_HEADS, MLA_QK)
    zq = lambda n: jnp.zeros((MLA_Q_LORA, MLA_HEADS, n), w_uq.dtype)
    x1, x2 = uq[..., 64:80], uq[..., 80:96]
    wq1 = jnp.concatenate([uq[..., :64], x1, x2, zq(32)], axis=-1).reshape(MLA_Q_LORA, 1024)
    wq2 = jnp.concatenate([zq(64), -x2, x1, zq(32)], axis=-1).reshape(MLA_Q_LORA, 1024)
    ukv = w_ukv.reshape(MLA_KV_LORA, MLA_HEADS, MLA_NOPE + MLA_V)
    wk = jnp.concatenate([ukv[..., :64], jnp.zeros((MLA_KV_LORA, MLA_HEADS, 64), w_ukv.dtype)],
                         axis=-1).reshape(MLA_KV_LORA, 1024)
    wv = ukv[..., 64:].reshape(MLA_KV_LORA, MLA_HEADS * MLA_V)
    return wq1.T.astype(BF16), wq2.T.astype(BF16), wk.astype(BF16), wv.T.astype(BF16)


def kernel(x, mem, positions, norm_mix, w_in, mla_q_norm, mla_w_uq, mla_kv_norm, mla_w_ukv, dn_conv, dn_a_log, dn_dt_bias, dn_out_norm, mem_norm, mem_w_kv, w_branch_mla, w_branch_dn, w_branch_mem, w_out, norm_ffn, ffn_w_gate_up, ffn_w_down, final_norm):
    b, s, d = x.shape
    t = b * s
    depth = w_in.shape[0]
    mt = mem.shape[1]
    ck, sk, cqt, sqt = _rope_tables(positions)
    gi = jnp.arange(DN_W) // DN_DK
    gmat = (gi[:, None] == gi[None, :]).astype(BF16)
    xf = x.reshape(t, d)
    memf = mem.reshape(b * mt, d)
    tl = _tiles(b, s)
    tm = tl.rows
    w_re = _reorder_w_in(w_in)
    memw, wbm, wbd, wbmem, wout = (a.astype(BF16) for a in (mem_w_kv, w_branch_mla, w_branch_dn, w_branch_mem, w_out))
    wgu, wdn = ffn_w_gate_up.astype(BF16), ffn_w_down.astype(BF16)
    for l in range(depth):
        wq1t, wq2t, wk, wvt = _mla_weights(mla_w_uq[l], mla_w_ukv[l])
        proj, small = _norm_matmul(xf, norm_mix[l][None], w_re, layer=l, tm=tl.proj_rows, tn=tl.proj_cols,
                                   tail=IN_COLS_P - COL_SMALL)

        qt, kk, vt = _mla_prep(proj, small, ck, sk, cqt, sqt, mla_q_norm[l][None], mla_kv_norm[l][None],
                               wq1t, wq2t, wk, wvt, b=b, s=s, tm=tm)
        o_mla = _flash(qt, kk, vt, b=b, s=s, tq=tl.flash_q, tk=tl.flash_kv, unroll=tl.flash_unroll)

        conv_w = jnp.concatenate([dn_conv[l], jnp.zeros((8 - DN_CONV, 3 * DN_W), F32)], axis=0)
        pad16 = lambda v: jnp.concatenate([v.reshape(-1), jnp.zeros((128 - 2 * DN_HEADS,), F32)])
        par = jnp.concatenate([pad16(dn_a_log[l])[None], pad16(dn_dt_bias[l])[None],
                               jnp.zeros((6, 128), F32)], axis=0)
        qn, kn, vv, gb = _dn_prep(proj, small, conv_w, par, gmat, b=b, s=s, tm=tm)
        o_f, o_b = _dn_chunk(qn, kn, vv, gb, b=b, s=s, nch=tl.dn_chunks)
        n_chunks = s // DN_CHUNK
        perm = lambda a: a.reshape(b, n_chunks, DN_CHUNK, DN_W).swapaxes(1, 2).reshape(t, DN_W)
        o_f, o_b = perm(o_f), perm(o_b)

        memkv = _norm_matmul(memf, mem_norm[l][None], memw, layer=l,
                             tm=b * mt, tn=512)[0].reshape(b, mt, 2 * MEM_W)
        onorm = jnp.tile(dn_out_norm[l], DN_HEADS)[None]
        xf = _merge(xf, proj, o_mla, o_f, o_b, memkv, onorm, gmat, wbm, wbd, wbmem, wout,
                    layer=l, b=b, s=s, tm=tm)
        xf = _ffn(xf, norm_ffn[l][None], wgu, wdn, final_norm[None],
                  layer=l, tm=tl.ffn_rows, nf=tl.ffn_slices, final_norm=(l == depth - 1))
    return xf.reshape(b, s, d)
```
